```python
import jax, jax.numpy as jnp
from jax import lax
import numpy as np

D_MODEL = 4096
BATCH = 2
SEQ = 8192
DEPTH = 1

MEM_LEN = 256
MIX_WIDTH = D_MODEL
RET_WIDTH = MIX_WIDTH // 2
GDN_WIDTH = MIX_WIDTH - RET_WIDTH
RET_HEAD_DIM = 256
RET_HEADS = RET_WIDTH // RET_HEAD_DIM
RET_CHUNK = 128
GDN_HEAD_DIM = 128
GDN_HEADS = GDN_WIDTH // GDN_HEAD_DIM
GDN_CHUNK = 64
CONV_WIDTH = 4
XATTN_HEADS = 4
XATTN_HEAD_DIM = D_MODEL // XATTN_HEADS
D_FF = 4 * D_MODEL
ROPE_THETA = 10000.0
NORM_EPS = 1e-6
IN_SPLITS = (RET_WIDTH, RET_WIDTH, RET_WIDTH, RET_WIDTH,
             GDN_WIDTH, GDN_WIDTH, GDN_WIDTH, GDN_WIDTH, GDN_HEADS, GDN_HEADS)
IN_WIDTH = 4 * RET_WIDTH + 4 * GDN_WIDTH + 2 * GDN_HEADS

kernel_name = "hymba_retention_gdn_xattn_layer"

F32 = jnp.float32


def rmsnorm(x, gain):
    xf = x.astype(F32)
    y = xf * lax.rsqrt(jnp.mean(xf * xf, axis=-1, keepdims=True) + NORM_EPS)
    return (y * gain.astype(F32)).astype(x.dtype)


def l2norm(x):
    xf = x.astype(F32)
    return xf * lax.rsqrt(jnp.sum(xf * xf, axis=-1, keepdims=True) + NORM_EPS)


def rotary(x, positions):
    half = x.shape[-1] // 2
    inv_freq = ROPE_THETA ** (-jnp.arange(half, dtype=F32) / half)
    ang = positions.astype(F32)[..., None] * inv_freq
    cos = jnp.cos(ang)[:, :, None, :]
    sin = jnp.sin(ang)[:, :, None, :]
    xf = x.astype(F32)
    x1, x2 = xf[..., :half], xf[..., half:]
    return jnp.concatenate([x1 * cos - x2 * sin, x2 * cos + x1 * sin], axis=-1)


def causal_short_conv(x, w):
    k_width, ch = w.shape
    return lax.conv_general_dilated(
        x, w[:, None, :].astype(x.dtype), window_strides=(1,),
        padding=[(k_width - 1, 0)], dimension_numbers=('NWC', 'WIO', 'NWC'),
        feature_group_count=ch)


def multiscale_retention(q, k, v, positions):
    B, S, H, Dh = q.shape
    C = RET_CHUNK
    N = S // C
    q = rotary(q, positions)
    k = rotary(k, positions) * (Dh ** -0.5)
    v = v.astype(F32)
    log_gamma = jnp.log1p(-jnp.exp2(-5.0 - jnp.arange(H, dtype=F32)))
    idx = jnp.arange(C, dtype=F32)
    rel = idx[:, None] - idx[None, :]
    causal = rel >= 0
    decay_intra = jnp.where(causal[None],
                            jnp.exp(jnp.maximum(rel, 0.0)[None] * log_gamma[:, None, None]),
                            0.0)
    decay_q = jnp.exp((idx + 1.0)[None, :] * log_gamma[:, None])
    decay_k = jnp.exp((C - 1.0 - idx)[None, :] * log_gamma[:, None])
    decay_chunk = jnp.exp(C * log_gamma)

    def to_chunks(t):
        return t.reshape(B, N, C, H, Dh).transpose(1, 0, 3, 2, 4)

    def step(state, xs):
        qc, kc, vc = xs
        scores = jnp.einsum('bhid,bhjd->bhij', qc, kc) * decay_intra
        out = (jnp.einsum('bhij,bhjd->bhid', scores, vc)
               + jnp.einsum('bhid,bhde->bhie', qc, state) * decay_q[..., None])
        state = (decay_chunk[:, None, None] * state
                 + jnp.einsum('bhjd,bhje->bhde', kc * decay_k[..., None], vc))
        return state, out

    state0 = jnp.zeros((B, H, Dh, Dh), F32)
    _, out = lax.scan(step, state0, (to_chunks(q), to_chunks(k), to_chunks(v)))
    return out.transpose(1, 0, 3, 2, 4).reshape(B, S, H, Dh)


def gated_delta_rule(q, k, v, log_decay, beta):
    B, S, H, D = q.shape
    C = GDN_CHUNK
    N = S // C
    q = l2norm(q) * (D ** -0.5)
    k = l2norm(k)
    v = v.astype(F32)

    def to_chunks(t):
        return t.reshape(B, N, C, H, D).transpose(0, 3, 1, 2, 4)

    def to_chunks_h(t):
        return t.reshape(B, N, C, H).transpose(0, 3, 1, 2)

    qc, kc, vc = to_chunks(q), to_chunks(k), to_chunks(v)
    g = jnp.cumsum(to_chunks_h(log_decay.astype(F32)), axis=-1)
    bc = to_chunks_h(beta.astype(F32))
    idx = jnp.arange(C)
    incl = idx[:, None] >= idx[None, :]
    strict = idx[:, None] > idx[None, :]
    diff = g[..., :, None] - g[..., None, :]
    decay_incl = jnp.where(incl, jnp.exp(jnp.where(incl, diff, 0.0)), 0.0)
    k_beta = kc * bc[..., None]
    v_beta = vc * bc[..., None]
    a_strict = jnp.where(strict, jnp.einsum('bhnid,bhnjd->bhnij', k_beta, kc) * decay_incl, 0.0)
    lhs = jnp.eye(C, dtype=F32) + a_strict
    rhs = jnp.concatenate([v_beta, k_beta * jnp.exp(g)[..., None]], axis=-1)
    sol = lax.linalg.triangular_solve(lhs, rhs, left_side=True, lower=True, unit_diagonal=True)
    u, w = sol[..., :D], sol[..., D:]
    qk = jnp.einsum('bhnid,bhnjd->bhnij', qc, kc) * decay_incl
    g_last = g[..., -1]
    q_dec = qc * jnp.exp(g)[..., None]
    k_dec = kc * jnp.exp(g_last[..., None] - g)[..., None]

    def step(state, xs):
        qk_n, q_dec_n, k_dec_n, u_n, w_n, gl_n = xs
        v_new = u_n - jnp.einsum('bhcd,bhde->bhce', w_n, state)
        out = (jnp.einsum('bhcd,bhde->bhce', q_dec_n, state)
               + jnp.einsum('bhij,bhje->bhie', qk_n, v_new))
        state = (jnp.exp(gl_n)[..., None, None] * state
                 + jnp.einsum('bhcd,bhce->bhde', k_dec_n, v_new))
        return state, out

    xs = tuple(jnp.moveaxis(t, 2, 0) for t in (qk, q_dec, k_dec, u, w, g_last))
    state0 = jnp.zeros((B, H, D, D), F32)
    _, out = lax.scan(step, state0, xs)
    return out.transpose(1, 0, 3, 2, 4).reshape(B, S, H, D)


def setup_inputs(seed: int = 0) -> dict:
    key = jax.random.key(seed)
    ks = jax.random.split(key, 24)
    L = DEPTH

    def dense(k, fan_in, fan_out):
        return jax.random.normal(k, (L, fan_in, fan_out), F32) * (fan_in ** -0.5)

    def gain(k, shape):
        return 1.0 + 0.02 * jax.random.normal(k, shape, F32)

    x = jax.random.normal(ks[0], (BATCH, SEQ, D_MODEL), F32)
    mem = jax.random.normal(ks[1], (BATCH, MEM_LEN, D_MODEL), F32)
    offset = jax.random.randint(ks[2], (BATCH, 1), 0, 4096, dtype=jnp.int32)
    positions = offset + jnp.arange(SEQ, dtype=jnp.int32)[None, :]
    dt = jnp.exp(jax.random.uniform(ks[3], (L, GDN_HEADS), F32, np.log(1e-3), np.log(1e-1)))
    return {
        "x": x,
        "mem": mem,
        "positions": positions,
        "mix_norm": gain(ks[4], (L, D_MODEL)),
        "w_in": dense(ks[5], D_MODEL, IN_WIDTH),
        "ret_norm": gain(ks[6], (L, RET_HEADS, RET_HEAD_DIM)),
        "gdn_conv": 0.5 * jax.random.normal(ks[7], (L, CONV_WIDTH, 3 * GDN_WIDTH), F32),
        "gdn_a_log": jnp.log(jax.random.uniform(ks[8], (L, GDN_HEADS), F32, 1.0, 16.0)),
        "gdn_dt_bias": dt + jnp.log(-jnp.expm1(-dt)),
        "gdn_norm": gain(ks[9], (L, GDN_HEAD_DIM)),
        "w_mix_out": dense(ks[10], MIX_WIDTH, D_MODEL),
        "xattn_norm": gain(ks[11], (L, D_MODEL)),
        "mem_norm": gain(ks[12], (L, D_MODEL)),
        "w_xq": dense(ks[13], D_MODEL, D_MODEL),
        "w_xk": dense(ks[14], D_MODEL, D_MODEL),
        "w_xv": dense(ks[15], D_MODEL, D_MODEL),
        "w_xo": dense(ks[16], D_MODEL, D_MODEL),
        "mlp_norm": gain(ks[17], (L, D_MODEL)),
        "w_up": dense(ks[18], D_MODEL, D_FF),
        "w_down": dense(ks[19], D_FF, D_MODEL),
        "final_norm": gain(ks[20], (D_MODEL,)),
    }


def reference(x, mem, positions, mix_norm, w_in, ret_norm, gdn_conv, gdn_a_log, gdn_dt_bias,
              gdn_norm, w_mix_out, xattn_norm, mem_norm, w_xq, w_xk, w_xv, w_xo,
              mlp_norm, w_up, w_down, final_norm):
    B, S, _ = x.shape
    M = mem.shape[1]
    split_points = np.cumsum(IN_SPLITS)[:-1]
    h = x
    for l in range(DEPTH):
        hn = rmsnorm(h, mix_norm[l])
        proj = hn @ w_in[l]
        rq, rk, rv, rg, gq, gk, gv, gz, ga, gb = jnp.split(proj, split_points, axis=-1)

        ret = multiscale_retention(rq.reshape(B, S, RET_HEADS, RET_HEAD_DIM),
                                   rk.reshape(B, S, RET_HEADS, RET_HEAD_DIM),
                                   rv.reshape(B, S, RET_HEADS, RET_HEAD_DIM), positions)
        ret = rmsnorm(ret, ret_norm[l]).reshape(B, S, RET_WIDTH)
        ret = jax.nn.silu(rg.astype(F32)) * ret

        qkv = jax.nn.silu(causal_short_conv(jnp.concatenate([gq, gk, gv], axis=-1), gdn_conv[l]))
        cq, ck, cv = jnp.split(qkv, 3, axis=-1)
        log_decay = -jnp.exp(gdn_a_log[l].astype(F32)) * jax.nn.softplus(
            ga.astype(F32) + gdn_dt_bias[l].astype(F32))
        beta = jax.nn.sigmoid(gb.astype(F32))
        gdn = gated_delta_rule(cq.reshape(B, S, GDN_HEADS, GDN_HEAD_DIM),
                               ck.reshape(B, S, GDN_HEADS, GDN_HEAD_DIM),
                               cv.reshape(B, S, GDN_HEADS, GDN_HEAD_DIM), log_decay, beta)
        gdn = rmsnorm(gdn, gdn_norm[l]) * jax.nn.silu(
            gz.reshape(B, S, GDN_HEADS, GDN_HEAD_DIM).astype(F32))

        mixed = jnp.concatenate([ret, gdn.reshape(B, S, GDN_WIDTH)], axis=-1).astype(h.dtype)
        h = h + mixed @ w_mix_out[l]

        hn = rmsnorm(h, xattn_norm[l])
        mn = rmsnorm(mem, mem_norm[l])
        q = (hn @ w_xq[l]).reshape(B, S, XATTN_HEADS, XATTN_HEAD_DIM)
        k = (mn @ w_xk[l]).reshape(B, M, XATTN_HEADS, XATTN_HEAD_DIM)
        v = (mn @ w_xv[l]).reshape(B, M, XATTN_HEADS, XATTN_HEAD_DIM)
        scores = jnp.einsum('bshd,bmhd->bhsm', q, k).astype(F32) * (XATTN_HEAD_DIM ** -0.5)
        probs = jax.nn.softmax(scores, axis=-1).astype(v.dtype)
        att = jnp.einsum('bhsm,bmhd->bshd', probs, v).reshape(B, S, D_MODEL)
        h = h + att @ w_xo[l]

        hn = rmsnorm(h, mlp_norm[l])
        h = h + jnp.square(jax.nn.relu(hn @ w_up[l])) @ w_down[l]
    return rmsnorm(h, final_norm)
```

```python
import functools
import math

import jax
import jax.numpy as jnp
import numpy as np
from jax import lax
from jax.experimental import pallas as pl
from jax.experimental.pallas import tpu as pltpu

F32 = jnp.float32
BF16 = jnp.bfloat16

NORM_EPS = 1e-6
ROPE_THETA = 10000.0

RET_HEAD_DIM = 256
GDN_HEAD_DIM = 128
GDN_CHUNK = 64
CONV_WIDTH = 4
XATTN_HEADS = 4

V7X_VMEM_LIMIT_BYTES = 56 * 1024 * 1024


def _params(*sem):
    return pltpu.CompilerParams(dimension_semantics=sem, vmem_limit_bytes=V7X_VMEM_LIMIT_BYTES)


def _dot(a, b):
    return jnp.dot(a, b, preferred_element_type=F32)


def _dot_nt(a, b):
    return lax.dot_general(a, b, (((1,), (1,)), ((), ())), preferred_element_type=F32)


def _dot_tn(a, b):
    return lax.dot_general(a, b, (((0,), (0,)), ((), ())), preferred_element_type=F32)


def _split3(a):
    hi = a.astype(BF16)
    lo = (a - hi.astype(F32)).astype(BF16)
    return hi, lo


def _dot3(a, b):
    ah, al = _split3(a)
    bh, bl = _split3(b)
    return _dot(ah, bh) + (_dot(ah, bl) + _dot(al, bh))


def _silu(x):
    return x * jax.nn.sigmoid(x)


def _rmsnorm_kernel(x_ref, g_ref, o_ref):
    x = x_ref[...]
    y = x * lax.rsqrt(jnp.mean(x * x, axis=-1, keepdims=True) + NORM_EPS)
    o_ref[...] = (y * g_ref[...]).astype(o_ref.dtype)


def _rmsnorm(x, gain, out_dtype, tm=512, name="rmsnorm"):
    t, d = x.shape
    tm = min(tm, t)
    return pl.pallas_call(
        _rmsnorm_kernel,
        grid=(t // tm,),
        in_specs=[pl.BlockSpec((tm, d), lambda i: (i, 0)),
                  pl.BlockSpec((1, d), lambda i: (0, 0))],
        out_specs=pl.BlockSpec((tm, d), lambda i: (i, 0)),
        out_shape=jax.ShapeDtypeStruct((t, d), out_dtype),
        compiler_params=_params("parallel"),
        name=name,
    )(x, gain.reshape(1, d))


def _mm_kernel(a_ref, w_ref, o_ref, *, relu2):
    acc = _dot(a_ref[...], w_ref[...])
    if relu2:
        acc = jnp.square(jnp.maximum(acc, 0.0))
    o_ref[...] = acc.astype(o_ref.dtype)


def _mm_res_kernel(a_ref, w_ref, r_ref, o_ref):
    o_ref[...] = r_ref[...] + _dot(a_ref[...], w_ref[...])


def _mm2_res_kernel(a1_ref, a2_ref, w_ref, r_ref, o_ref):
    k1 = a1_ref.shape[1]
    acc = _dot(a1_ref[...], w_ref[:k1, :]) + _dot(a2_ref[...], w_ref[k1:, :])
    o_ref[...] = r_ref[...] + acc


def _mm_acc_res_kernel(a_ref, w_ref, r_ref, o_ref, acc_ref):
    k = pl.program_id(2)

    @pl.when(k == 0)
    def _():
        acc_ref[...] = r_ref[...]

    acc_ref[...] += _dot(a_ref[...], w_ref[...])

    @pl.when(k == pl.num_programs(2) - 1)
    def _():
        o_ref[...] = acc_ref[...]


def _matmul(a, w, out_dtype, *, relu2=False, tm=1024, tn=512, name="matmul"):
    m, k = a.shape
    n = w.shape[1]
    tm, tn = min(tm, m), min(tn, n)
    return pl.pallas_call(
        functools.partial(_mm_kernel, relu2=relu2),
        grid=(m // tm, n // tn),
        in_specs=[pl.BlockSpec((tm, k), lambda i, j: (i, 0)),
                  pl.BlockSpec((k, tn), lambda i, j: (0, j))],
        out_specs=pl.BlockSpec((tm, tn), lambda i, j: (i, j)),
        out_shape=jax.ShapeDtypeStruct((m, n), out_dtype),
        compiler_params=_params("parallel", "arbitrary"),
        name=name,
    )(a, w)


def _matmul_res(a, w, res, *, tm=1024, tn=512, name="matmul_res"):
    m, k = a.shape
    n = w.shape[1]
    tm, tn = min(tm, m), min(tn, n)
    return pl.pallas_call(
        _mm_res_kernel,
        grid=(m // tm, n // tn),
        in_specs=[pl.BlockSpec((tm, k), lambda i, j: (i, 0)),
                  pl.BlockSpec((k, tn), lambda i, j: (0, j)),
                  pl.BlockSpec((tm, tn), lambda i, j: (i, j))],
        out_specs=pl.BlockSpec((tm, tn), lambda i, j: (i, j)),
        out_shape=jax.ShapeDtypeStruct((m, n), F32),
        compiler_params=_params("parallel", "arbitrary"),
        name=name,
    )(a, w, res)


def _matmul2_res(a1, a2, w, res, *, tm=1024, tn=512, name="matmul2_res"):
    m, k1 = a1.shape
    k2 = a2.shape[1]
    n = w.shape[1]
    tm, tn = min(tm, m), min(tn, n)
    return pl.pallas_call(
        _mm2_res_kernel,
        grid=(m // tm, n // tn),
        in_specs=[pl.BlockSpec((tm, k1), lambda i, j: (i, 0)),
                  pl.BlockSpec((tm, k2), lambda i, j: (i, 0)),
                  pl.BlockSpec((k1 + k2, tn), lambda i, j: (0, j)),
                  pl.BlockSpec((tm, tn), lambda i, j: (i, j))],
        out_specs=pl.BlockSpec((tm, tn), lambda i, j: (i, j)),
        out_shape=jax.ShapeDtypeStruct((m, n), F32),
        compiler_params=_params("parallel", "arbitrary"),
        name=name,
    )(a1, a2, w, res)


def _matmul_acc_res(a, w, res, *, tm=1024, tn=1024, tk=1024, name="matmul_acc_res"):
    m, k = a.shape
    n = w.shape[1]
    tm, tn, tk = min(tm, m), min(tn, n), min(tk, k)
    return pl.pallas_call(
        _mm_acc_res_kernel,
        grid=(m // tm, n // tn, k // tk),
        in_specs=[pl.BlockSpec((tm, tk), lambda i, j, l: (i, l)),
                  pl.BlockSpec((tk, tn), lambda i, j, l: (l, j)),
                  pl.BlockSpec((tm, tn), lambda i, j, l: (i, j))],
        out_specs=pl.BlockSpec((tm, tn), lambda i, j, l: (i, j)),
        out_shape=jax.ShapeDtypeStruct((m, n), F32),
        scratch_shapes=[pltpu.VMEM((tm, tn), F32)],
        compiler_params=_params("parallel", "parallel", "arbitrary"),
        name=name,
    )(a, w, res)


def _retention_kernel(lg_ref, pos_ref, freq_ref, q_ref, k_ref, v_ref, g_ref, rn_ref,
                      o_ref, state_ref, cos_ref, sin_ref):
    n = pl.program_id(1)
    h = pl.program_id(2)
    c, dh = q_ref.shape[1], q_ref.shape[2]
    half = dh // 2

    @pl.when(h == 0)
    def _():
        ang = pos_ref[0].astype(F32) * freq_ref[...]
        cos_ref[...] = jnp.cos(ang)
        sin_ref[...] = jnp.sin(ang)

    @pl.when(n == 0)
    def _():
        state_ref[h] = jnp.zeros((dh, dh), F32)

    cos = cos_ref[...]
    sin = sin_ref[...]

    def rot(x):
        x1, x2 = x[:, :half], x[:, half:]
        return jnp.concatenate([x1 * cos - x2 * sin, x2 * cos + x1 * sin], axis=-1)

    lg = jnp.full((1, 1), lg_ref[h], F32)
    q = rot(q_ref[0])
    k = rot(k_ref[0]) * (dh ** -0.5)
    v = v_ref[0]

    row = lax.broadcasted_iota(jnp.int32, (c, c), 0)
    col = lax.broadcasted_iota(jnp.int32, (c, c), 1)
    rel = (row - col).astype(F32)
    decay_intra = jnp.where(rel >= 0, jnp.exp(jnp.maximum(rel, 0.0) * lg), 0.0)
    idx = lax.broadcasted_iota(jnp.int32, (c, 1), 0).astype(F32)
    decay_q = jnp.exp((idx + 1.0) * lg)
    decay_k = jnp.exp((c - 1.0 - idx) * lg)
    decay_chunk = jnp.exp(c * lg)

    qb = q.astype(BF16)
    vb = v.astype(BF16)
    state = state_ref[h]
    scores = _dot_nt(qb, k.astype(BF16)) * decay_intra
    out = _dot(scores.astype(BF16), vb) + _dot(qb, state.astype(BF16)) * decay_q
    state_ref[h] = decay_chunk * state + _dot_tn((k * decay_k).astype(BF16), vb)

    y = out * lax.rsqrt(jnp.mean(out * out, axis=-1, keepdims=True) + NORM_EPS)
    y = y * rn_ref[0]
    o_ref[0] = (_silu(g_ref[0]) * y).astype(o_ref.dtype)


def _retention(proj, positions, ret_norm, *, heads, chunk=256):
    b, s, _ = proj.shape
    dh = RET_HEAD_DIM
    half = dh // 2
    chunk = min(chunk, s)
    log_gamma = jnp.log1p(-jnp.exp2(-5.0 - jnp.arange(heads, dtype=F32)))
    inv_freq = (ROPE_THETA ** (-jnp.arange(half, dtype=F32) / half)).reshape(1, half)
    blk = lambda off: pl.BlockSpec((1, chunk, dh), lambda bi, ni, hi, lg: (bi, ni, off + hi))
    grid_spec = pltpu.PrefetchScalarGridSpec(
        num_scalar_prefetch=1,
        grid=(b, s // chunk, heads),
        in_specs=[pl.BlockSpec((1, chunk, 1), lambda bi, ni, hi, lg: (bi, ni, 0)),
                  pl.BlockSpec((1, half), lambda bi, ni, hi, lg: (0, 0)),
                  blk(0), blk(heads), blk(2 * heads), blk(3 * heads),
                  pl.BlockSpec((1, 1, dh), lambda bi, ni, hi, lg: (hi, 0, 0))],
        out_specs=pl.BlockSpec((1, chunk, dh), lambda bi, ni, hi, lg: (bi, ni, hi)),
        scratch_shapes=[pltpu.VMEM((heads, dh, dh), F32),
                        pltpu.VMEM((chunk, half), F32),
                        pltpu.VMEM((chunk, half), F32)],
    )
    return pl.pallas_call(
        _retention_kernel,
        grid_spec=grid_spec,
        out_shape=jax.ShapeDtypeStruct((b, s, heads * dh), BF16),
        compiler_params=_params("arbitrary", "arbitrary", "arbitrary"),
        name="retention",
    )(log_gamma, positions.reshape(b, s, 1), inv_freq, proj, proj, proj, proj,
      ret_norm.reshape(heads, 1, dh))


def _block_tril(n, blk, dtype):
    row = lax.broadcasted_iota(jnp.int32, (n, n), 0)
    col = lax.broadcasted_iota(jnp.int32, (n, n), 1)
    return jnp.where((row >= col) & (row // blk == col // blk), 1.0, 0.0).astype(dtype)


def _softplus(x):
    return jnp.maximum(x, 0.0) + jnp.log1p(jnp.exp(-jnp.abs(x)))


def _gdn_gates_kernel(hn_ref, w_ref, wt_ref, a_ref, dt_ref, at_ref, dtt_ref, gcol_ref, grow_ref):
    nh = a_ref.shape[1]
    tm = hn_ref.shape[0]
    hn = hn_ref[...]
    ab = _dot(hn, w_ref[...])
    abt = _dot_nt(wt_ref[...], hn)
    ltri = _block_tril(tm, GDN_CHUNK, F32)
    ld = -jnp.exp(a_ref[...]) * _softplus(ab[:, :nh] + dt_ref[...])
    ldt = -jnp.exp(at_ref[...]) * _softplus(abt[:nh, :] + dtt_ref[...])
    g = jnp.dot(ltri, ld, preferred_element_type=F32, precision=lax.Precision.HIGHEST)
    gt = lax.dot_general(ldt, ltri, (((1,), (1,)), ((), ())), preferred_element_type=F32,
                         precision=lax.Precision.HIGHEST)
    gcol_ref[...] = jnp.concatenate([g, jax.nn.sigmoid(ab[:, nh:])], axis=-1)
    grow_ref[...] = gt


def _gdn_gates(hn, w_ab, a_log, dt_bias, *, tm=256):
    t, d = hn.shape
    nh = a_log.shape[0]
    tm = min(tm, t)
    const = lambda shape: pl.BlockSpec(shape, lambda i: (0, 0))
    return pl.pallas_call(
        _gdn_gates_kernel,
        grid=(t // tm,),
        in_specs=[pl.BlockSpec((tm, d), lambda i: (i, 0)),
                  const((d, 2 * nh)), const((2 * nh, d)),
                  const((1, nh)), const((1, nh)), const((nh, 1)), const((nh, 1))],
        out_specs=[pl.BlockSpec((tm, 2 * nh), lambda i: (i, 0)),
                   pl.BlockSpec((nh, tm), lambda i: (0, i))],
        out_shape=[jax.ShapeDtypeStruct((t, 2 * nh), F32),
                   jax.ShapeDtypeStruct((nh, t), F32)],
        compiler_params=_params("parallel"),
        name="gdn_gates",
    )(hn, w_ab, w_ab.T, a_log.reshape(1, nh), dt_bias.reshape(1, nh),
      a_log.reshape(nh, 1), dt_bias.reshape(nh, 1))


def _gdn_kernel(q_ref, k_ref, v_ref, z_ref, gcol_ref, grow_ref, cwq_ref, cwk_ref, cwv_ref,
                gn_ref, o_ref, state_ref, carry_ref, xext_ref, vnew_ref, qs_ref):
    n = pl.program_id(1)
    h = pl.program_id(2)
    tb, d = q_ref.shape[1], q_ref.shape[2]
    nh = grow_ref.shape[0]
    c = GDN_CHUNK
    pad = 8

    @pl.when(n == 0)
    def _():
        state_ref[h] = jnp.zeros((d, d), F32)
        carry_ref[h] = jnp.zeros((3, pad, d), F32)

    def conv_silu(x_ref, w_ref, slot):
        x = x_ref[0]
        xext_ref[:pad, :] = carry_ref[h, slot]
        xext_ref[pad:, :] = x
        carry_ref[h, slot] = x[tb - pad:, :]
        w = w_ref[...]
        acc = x * w[CONV_WIDTH - 1:CONV_WIDTH, :]
        for s in range(1, CONV_WIDTH):
            acc = acc + xext_ref[pad - s:pad - s + tb, :] * w[CONV_WIDTH - 1 - s:CONV_WIDTH - s, :]
        return _silu(acc)

    cq = conv_silu(q_ref, cwq_ref, 0)
    ck = conv_silu(k_ref, cwk_ref, 1)
    v = conv_silu(v_ref, cwv_ref, 2)
    q = cq * lax.rsqrt(jnp.sum(cq * cq, axis=-1, keepdims=True) + NORM_EPS) * (d ** -0.5)
    k = ck * lax.rsqrt(jnp.sum(ck * ck, axis=-1, keepdims=True) + NORM_EPS)

    gcol = gcol_ref[0]
    lane = lax.broadcasted_iota(jnp.int32, gcol.shape, 1)
    g_col = jnp.sum(jnp.where(lane == h, gcol, 0.0), axis=-1, keepdims=True)
    beta = jnp.sum(jnp.where(lane == nh + h, gcol, 0.0), axis=-1, keepdims=True)
    g_row = grow_ref[pl.ds(h, 1), :]

    row = lax.broadcasted_iota(jnp.int32, (tb, tb), 0)
    col = lax.broadcasted_iota(jnp.int32, (tb, tb), 1)
    same = (row // c) == (col // c)
    incl = same & (row >= col)
    strict = same & (row > col)
    decay = jnp.where(incl, jnp.exp(jnp.where(incl, g_col - g_row, 0.0)), 0.0)

    eg = jnp.exp(g_col)
    k_beta = k * beta
    v_beta = v * beta
    kb = k.astype(BF16)
    kk = _dot_nt(k_beta.astype(BF16), kb)
    qk = _dot_nt(q.astype(BF16), kb) * decay
    a = jnp.where(strict, kk * decay, 0.0)

    eye = jnp.where(row == col, 1.0, 0.0)
    inv = eye - a
    p = a
    for _ in range(int(math.log2(c)) - 1):
        p = _dot3(p, p)
        inv = inv + _dot3(inv, p)

    rhs = jnp.concatenate([v_beta, k_beta * eg], axis=-1)
    sol = _dot3(inv, rhs)
    u, w = sol[:, :d], sol[:, d:]
    q_dec = q * eg

    state = state_ref[h]
    for ci in range(tb // c):
        r0 = ci * c
        g_c = g_col[r0:r0 + c]
        g_last = g_col[r0 + c - 1:r0 + c]
        sb = state.astype(BF16)
        lhs = jnp.concatenate([w[r0:r0 + c], q_dec[r0:r0 + c]], axis=0).astype(BF16)
        ws = _dot(lhs, sb)
        v_new = u[r0:r0 + c] - ws[:c]
        vnew_ref[r0:r0 + c, :] = v_new
        qs_ref[r0:r0 + c, :] = ws[c:]
        k_dec = k[r0:r0 + c] * jnp.exp(g_last - g_c)
        state = jnp.exp(g_last) * state + _dot_tn(k_dec.astype(BF16), v_new.astype(BF16))
    state_ref[h] = state

    out = qs_ref[...] + _dot(qk.astype(BF16), vnew_ref[...].astype(BF16))
    y = out * lax.rsqrt(jnp.mean(out * out, axis=-1, keepdims=True) + NORM_EPS)
    o_ref[0] = (y * gn_ref[...] * _silu(z_ref[0])).astype(o_ref.dtype)


def _gdn(proj, gcol, grow, conv_w, gdn_norm, *, heads, col0, tb=256):
    b, s, _ = proj.shape
    d = GDN_HEAD_DIM
    tb = min(tb, s)
    c0 = col0 // d
    blk = lambda off: pl.BlockSpec((1, tb, d), lambda bi, ni, hi: (bi, ni, c0 + off + hi))
    cw = lambda off: pl.BlockSpec((CONV_WIDTH, d), lambda bi, ni, hi: (0, off + hi))
    return pl.pallas_call(
        _gdn_kernel,
        grid=(b, s // tb, heads),
        in_specs=[blk(0), blk(heads), blk(2 * heads), blk(3 * heads),
                  pl.BlockSpec((1, tb, 2 * heads), lambda bi, ni, hi: (bi, ni, 0)),
                  pl.BlockSpec((heads, tb), lambda bi, ni, hi: (0, bi * (s // tb) + ni)),
                  cw(0), cw(heads), cw(2 * heads),
                  pl.BlockSpec((1, d), lambda bi, ni, hi: (0, 0))],
        out_specs=pl.BlockSpec((1, tb, d), lambda bi, ni, hi: (bi, ni, hi)),
        out_shape=jax.ShapeDtypeStruct((b, s, heads * d), BF16),
        scratch_shapes=[pltpu.VMEM((heads, d, d), F32),
                        pltpu.VMEM((heads, 3, 8, d), F32),
                        pltpu.VMEM((tb + 8, d), F32),
                        pltpu.VMEM((tb, d), F32),
                        pltpu.VMEM((tb, d), F32)],
        compiler_params=_params("arbitrary", "arbitrary", "arbitrary"),
        name="gdn",
    )(proj, proj, proj, proj, gcol, grow, conv_w, conv_w, conv_w, gdn_norm.reshape(1, d))


def _xattn_kernel(hn_ref, wq_ref, k_ref, v_ref, o_ref):
    dh = wq_ref.shape[1]
    q = _dot(hn_ref[0], wq_ref[...])
    scores = _dot_nt(q.astype(BF16), k_ref[0]) * (dh ** -0.5)
    m = jnp.max(scores, axis=-1, keepdims=True)
    e = jnp.exp(scores - m)
    probs = e / jnp.sum(e, axis=-1, keepdims=True)
    o_ref[0] = _dot(probs.astype(BF16), v_ref[0]).astype(o_ref.dtype)


def _xattn(hn, w_xq, k, v, *, heads, tm=512):
    b, s, d = hn.shape
    m = k.shape[1]
    dh = d // heads
    tm = min(tm, s)
    return pl.pallas_call(
        _xattn_kernel,
        grid=(heads, b, s // tm),
        in_specs=[pl.BlockSpec((1, tm, d), lambda hi, bi, i: (bi, i, 0)),
                  pl.BlockSpec((d, dh), lambda hi, bi, i: (0, hi)),
                  pl.BlockSpec((1, m, dh), lambda hi, bi, i: (bi, 0, hi)),
                  pl.BlockSpec((1, m, dh), lambda hi, bi, i: (bi, 0, hi))],
        out_specs=pl.BlockSpec((1, tm, dh), lambda hi, bi, i: (bi, i, hi)),
        out_shape=jax.ShapeDtypeStruct((b, s, d), BF16),
        compiler_params=_params("parallel", "parallel", "arbitrary"),
        name="xattn",
    )(hn, w_xq, k, v)


def kernel(x, mem, positions, mix_norm, w_in, ret_norm, gdn_conv, gdn_a_log, gdn_dt_bias, gdn_norm,
           w_mix_out, xattn_norm, mem_norm, w_xq, w_xk, w_xv, w_xo, mlp_norm, w_up, w_down, final_norm):
    b, s, d = x.shape
    m = mem.shape[1]
    t = b * s
    depth = w_in.shape[0]
    ret_heads = ret_norm.shape[1]
    gdn_heads = gdn_a_log.shape[1]
    ret_width = ret_heads * RET_HEAD_DIM
    gdn_width = gdn_heads * GDN_HEAD_DIM
    main_width = 4 * ret_width + 4 * gdn_width

    h = x.reshape(t, d)
    for l in range(depth):
        hn = _rmsnorm(h, mix_norm[l], BF16, name="mix_rmsnorm")
        w_in_l = w_in[l].astype(BF16)
        proj = _matmul(hn, w_in_l[:, :main_width], F32, name="in_proj")
        gcol, grow = _gdn_gates(hn, w_in_l[:, main_width:], gdn_a_log[l], gdn_dt_bias[l])
        proj3 = proj.reshape(b, s, main_width)
        ret = _retention(proj3, positions, ret_norm[l], heads=ret_heads)
        gdn = _gdn(proj3, gcol.reshape(b, s, 2 * gdn_heads), grow, gdn_conv[l], gdn_norm[l], heads=gdn_heads, col0=4 * ret_width)
        h = _matmul2_res(ret.reshape(t, ret_width), gdn.reshape(t, gdn_width),
                         w_mix_out[l].astype(BF16), h, name="mix_out")

        hn = _rmsnorm(h, xattn_norm[l], BF16, name="xattn_rmsnorm")
        mn = _rmsnorm(mem.reshape(b * m, d), mem_norm[l], BF16, name="mem_rmsnorm")
        k = _matmul(mn, w_xk[l].astype(BF16), BF16, name="xattn_k")
        v = _matmul(mn, w_xv[l].astype(BF16), BF16, name="xattn_v")
        att = _xattn(hn.reshape(b, s, d), w_xq[l].astype(BF16), k.reshape(b, m, d),
                     v.reshape(b, m, d), heads=XATTN_HEADS)
        h = _matmul_res(att.reshape(t, d), w_xo[l].astype(BF16), h, name="xattn_out")

        hn = _rmsnorm(h, mlp_norm[l], BF16, name="mlp_rmsnorm")
        up = _matmul(hn, w_up[l].astype(BF16), BF16, relu2=True, name="mlp_up")
        h = _matmul_acc_res(up, w_down[l].astype(BF16), h, name="mlp_down")
    out = _rmsnorm(h, final_norm, x.dtype, name="final_rmsnorm")
    return out.reshape(b, s, d)
```

```python
import functools
import math

import jax
import jax.numpy as jnp
from jax import lax
from jax.experimental import pallas as pl
from jax.experimental.pallas import tpu as pltpu

F32 = jnp.float32
BF16 = jnp.bfloat16

NORM_EPS = 1e-6
ROPE_THETA = 10000.0

RET_HEAD_DIM = 256
GDN_HEAD_DIM = 128
GDN_CHUNK = 64
CONV_WIDTH = 4
XATTN_HEADS = 4
SUBLANES = 8
MATMUL_TM = 1024
MATMUL_TN = 512
CONV_ROW_CHUNK = 256

V7X_VMEM_LIMIT_BYTES = 56 * 1024 * 1024


def _params(*sem):
    return pltpu.CompilerParams(dimension_semantics=sem, vmem_limit_bytes=V7X_VMEM_LIMIT_BYTES)


def _dot(a, b):
    return jnp.dot(a, b, preferred_element_type=F32)


def _dot_nt(a, b):
    return lax.dot_general(a, b, (((1,), (1,)), ((), ())), preferred_element_type=F32)


def _dot_tn(a, b):
    return lax.dot_general(a, b, (((0,), (0,)), ((), ())), preferred_element_type=F32)


def _silu(x):
    return x * jax.nn.sigmoid(x)


def _rmsnorm_kernel(x_ref, g_ref, o_ref):
    x = x_ref[...]
    y = x * lax.rsqrt(jnp.mean(x * x, axis=-1, keepdims=True) + NORM_EPS)
    o_ref[...] = (y * g_ref[...]).astype(o_ref.dtype)


def _rmsnorm(x, gain, out_dtype, tm=512, name="rmsnorm"):
    t, d = x.shape
    tm = min(tm, t)
    return pl.pallas_call(
        _rmsnorm_kernel,
        grid=(t // tm,),
        in_specs=[pl.BlockSpec((tm, d), lambda i: (i, 0)),
                  pl.BlockSpec((1, d), lambda i: (0, 0))],
        out_specs=pl.BlockSpec((tm, d), lambda i: (i, 0)),
        out_shape=jax.ShapeDtypeStruct((t, d), out_dtype),
        compiler_params=_params("parallel"),
        name=name,
    )(x, gain.reshape(1, d))


def _mm_kernel(*refs, n_a, has_res, relu2, conv):
    a_refs = refs[:n_a]
    w_ref = refs[n_a]
    r_ref = refs[n_a + 1] if has_res else None
    cw_ref = refs[n_a + 1] if conv is not None else None
    if conv is None:
        o_ref, wb_ref = refs[-2], refs[-1]
    else:
        o_ref, wb_ref, xext_ref = refs[-3], refs[-2], refs[-1]
    i = pl.program_id(1)

    @pl.when(i == 0)
    def _():
        wb_ref[...] = w_ref[...].astype(BF16)

    if conv is None:
        k0 = 0
        acc = None
        for a_ref in a_refs:
            k1 = k0 + a_ref.shape[1]
            part = _dot(a_ref[...], wb_ref[k0:k1, :])
            acc = part if acc is None else acc + part
            k0 = k1
        if relu2:
            acc = jnp.square(jnp.maximum(acc, 0.0))
        if has_res:
            acc = r_ref[...] + acc
        o_ref[...] = acc.astype(o_ref.dtype)
        return

    seq_len, l2_head_dim, n_scaled_tiles, scale = conv
    (a_ref,) = a_refs
    tm, tn = o_ref.shape
    pad = SUBLANES
    rc = min(CONV_ROW_CHUNK, tm)

    @pl.when((i * tm) % seq_len == 0)
    def _():
        xext_ref[:pad, :] = jnp.zeros((pad, tn), F32)

    cw = cw_ref[...]
    mult = jnp.where(pl.program_id(0) < n_scaled_tiles, scale, 1.0)
    for r0 in range(0, tm, rc):
        acc = _dot(a_ref[r0:r0 + rc, :], wb_ref[...])
        xext_ref[pad + r0:pad + r0 + rc, :] = acc
        y = acc * cw[CONV_WIDTH - 1:CONV_WIDTH, :]
        for s in range(1, CONV_WIDTH):
            y = y + xext_ref[pad + r0 - s:pad + r0 - s + rc, :] * cw[CONV_WIDTH - 1 - s:CONV_WIDTH - s, :]
        y = _silu(y)
        if l2_head_dim:
            parts = []
            for c0 in range(0, tn, l2_head_dim):
                yh = y[:, c0:c0 + l2_head_dim]
                parts.append(yh * lax.rsqrt(jnp.sum(yh * yh, axis=-1, keepdims=True) + NORM_EPS) * mult)
            y = jnp.concatenate(parts, axis=-1)
        o_ref[r0:r0 + rc, :] = y.astype(o_ref.dtype)
    xext_ref[:pad, :] = xext_ref[tm:tm + pad, :]


def _matmul(a_list, w, layer, out_dtype, *, n_tiles=None, w_tile_of=lambda j: j, res=None, relu2=False,
            conv=None, conv_w=None, conv_tile0=0, tm=MATMUL_TM, tn=MATMUL_TN, name="matmul"):
    m = a_list[0].shape[0]
    k = sum(a.shape[1] for a in a_list)
    assert w.shape[1] == k
    tm, tn = min(tm, m), min(tn, w.shape[2])
    if n_tiles is None:
        assert w.shape[2] % tn == 0
        n_tiles = w.shape[2] // tn
    assert m % tm == 0
    in_specs = [pl.BlockSpec((tm, a.shape[1]), lambda j, i: (i, 0)) for a in a_list]
    in_specs.append(pl.BlockSpec((None, k, tn), lambda j, i: (layer, 0, w_tile_of(j))))
    args = list(a_list) + [w]
    scratch = [pltpu.VMEM((k, tn), BF16)]
    if res is not None:
        in_specs.append(pl.BlockSpec((tm, tn), lambda j, i: (i, j)))
        args.append(res)
    if conv is not None:
        assert res is None and conv[0] % tm == 0
        in_specs.append(pl.BlockSpec((CONV_WIDTH, tn), lambda j, i: (0, conv_tile0 + j)))
        args.append(conv_w)
        scratch.append(pltpu.VMEM((tm + SUBLANES, tn), F32))
    return pl.pallas_call(
        functools.partial(_mm_kernel, n_a=len(a_list), has_res=res is not None, relu2=relu2, conv=conv),
        grid=(n_tiles, m // tm),
        in_specs=in_specs,
        out_specs=pl.BlockSpec((tm, tn), lambda j, i: (i, j)),
        out_shape=jax.ShapeDtypeStruct((m, n_tiles * tn), out_dtype),
        scratch_shapes=scratch,
        compiler_params=_params("parallel", "arbitrary"),
        name=name,
    )(*args)


def _mm_acc_res_kernel(a_ref, w_ref, r_ref, o_ref, acc_ref):
    k = pl.program_id(2)

    @pl.when(k == 0)
    def _():
        acc_ref[...] = r_ref[...]

    acc_ref[...] += _dot(a_ref[...], w_ref[...])

    @pl.when(k == pl.num_programs(2) - 1)
    def _():
        o_ref[...] = acc_ref[...]


def _matmul_acc_res(a, w, res, *, tm=1024, tn=1024, tk=2048, name="matmul_acc_res"):
    m, k = a.shape
    n = w.shape[1]
    tm, tn, tk = min(tm, m), min(tn, n), min(tk, k)
    return pl.pallas_call(
        _mm_acc_res_kernel,
        grid=(m // tm, n // tn, k // tk),
        in_specs=[pl.BlockSpec((tm, tk), lambda i, j, l: (i, l)),
                  pl.BlockSpec((tk, tn), lambda i, j, l: (l, j)),
                  pl.BlockSpec((tm, tn), lambda i, j, l: (i, j))],
        out_specs=pl.BlockSpec((tm, tn), lambda i, j, l: (i, j)),
        out_shape=jax.ShapeDtypeStruct((m, n), F32),
        scratch_shapes=[pltpu.VMEM((tm, tn), F32)],
        compiler_params=_params("parallel", "parallel", "arbitrary"),
        name=name,
    )(a, w, res)


def _retention_kernel(lg_ref, pos_ref, freq_ref, q_ref, k_ref, v_ref, g_ref, rn_ref,
                      o_ref, state_ref, cos_ref, sin_ref):
    n = pl.program_id(1)
    h = pl.program_id(2)
    c, dh = q_ref.shape[1], q_ref.shape[2]
    half = dh // 2

    @pl.when(h == 0)
    def _():
        ang = pos_ref[0].astype(F32) * freq_ref[...]
        cos_ref[...] = jnp.cos(ang)
        sin_ref[...] = jnp.sin(ang)

    @pl.when(n == 0)
    def _():
        state_ref[h] = jnp.zeros((dh, dh), F32)

    cos = cos_ref[...]
    sin = sin_ref[...]

    def rot(x):
        x1, x2 = x[:, :half], x[:, half:]
        return jnp.concatenate([x1 * cos - x2 * sin, x2 * cos + x1 * sin], axis=-1)

    lg = jnp.full((1, 1), lg_ref[h], F32)
    q = rot(q_ref[0])
    k = rot(k_ref[0]) * (dh ** -0.5)
    v = v_ref[0]

    row = lax.broadcasted_iota(jnp.int32, (c, c), 0)
    col = lax.broadcasted_iota(jnp.int32, (c, c), 1)
    rel = (row - col).astype(F32)
    decay_intra = jnp.where(rel >= 0, jnp.exp(jnp.maximum(rel, 0.0) * lg), 0.0)
    idx = lax.broadcasted_iota(jnp.int32, (c, 1), 0).astype(F32)
    decay_q = jnp.exp((idx + 1.0) * lg)
    decay_k = jnp.exp((c - 1.0 - idx) * lg)
    decay_chunk = jnp.exp(c * lg)

    qb = q.astype(BF16)
    vb = v.astype(BF16)
    state = state_ref[h]
    scores = _dot_nt(qb, k.astype(BF16)) * decay_intra
    out = _dot(scores.astype(BF16), vb) + _dot(qb, state.astype(BF16)) * decay_q
    state_ref[h] = decay_chunk * state + _dot_tn((k * decay_k).astype(BF16), vb)

    y = out * lax.rsqrt(jnp.mean(out * out, axis=-1, keepdims=True) + NORM_EPS)
    y = y * rn_ref[0]
    o_ref[0] = (_silu(g_ref[0]) * y).astype(o_ref.dtype)


def _retention(proj, positions, ret_norm, *, heads, chunk=256):
    b, s, _ = proj.shape
    dh = RET_HEAD_DIM
    half = dh // 2
    chunk = min(chunk, s)
    log_gamma = jnp.log1p(-jnp.exp2(-5.0 - jnp.arange(heads, dtype=F32)))
    inv_freq = (ROPE_THETA ** (-jnp.arange(half, dtype=F32) / half)).reshape(1, half)
    blk = lambda off: pl.BlockSpec((1, chunk, dh), lambda bi, ni, hi, lg: (bi, ni, off + hi))
    grid_spec = pltpu.PrefetchScalarGridSpec(
        num_scalar_prefetch=1,
        grid=(b, s // chunk, heads),
        in_specs=[pl.BlockSpec((1, chunk, 1), lambda bi, ni, hi, lg: (bi, ni, 0)),
                  pl.BlockSpec((1, half), lambda bi, ni, hi, lg: (0, 0)),
                  blk(0), blk(heads), blk(2 * heads), blk(3 * heads),
                  pl.BlockSpec((1, 1, dh), lambda bi, ni, hi, lg: (hi, 0, 0))],
        out_specs=pl.BlockSpec((1, chunk, dh), lambda bi, ni, hi, lg: (bi, ni, hi)),
        scratch_shapes=[pltpu.VMEM((heads, dh, dh), F32),
                        pltpu.VMEM((chunk, half), F32),
                        pltpu.VMEM((chunk, half), F32)],
    )
    return pl.pallas_call(
        _retention_kernel,
        grid_spec=grid_spec,
        out_shape=jax.ShapeDtypeStruct((b, s, heads * dh), BF16),
        compiler_params=_params("arbitrary", "arbitrary", "arbitrary"),
        name="retention",
    )(log_gamma, positions.reshape(b, s, 1), inv_freq, proj, proj, proj, proj,
      ret_norm.reshape(heads, 1, dh))


def _block_tril(n, blk, dtype):
    row = lax.broadcasted_iota(jnp.int32, (n, n), 0)
    col = lax.broadcasted_iota(jnp.int32, (n, n), 1)
    return jnp.where((row >= col) & (row // blk == col // blk), 1.0, 0.0).astype(dtype)


def _softplus(x):
    return jnp.maximum(x, 0.0) + jnp.log1p(jnp.exp(-jnp.abs(x)))


def _gdn_gates_kernel(hn_ref, w_ref, wt_ref, a_ref, dt_ref, at_ref, dtt_ref, gcol_ref, grow_ref):
    nh = a_ref.shape[1]
    tm = hn_ref.shape[0]
    hn = hn_ref[...]
    ab = _dot(hn, w_ref[...])
    abt = _dot_nt(wt_ref[...], hn)
    ltri = _block_tril(tm, GDN_CHUNK, F32)
    ld = -jnp.exp(a_ref[...]) * _softplus(ab[:, :nh] + dt_ref[...])
    ldt = -jnp.exp(at_ref[...]) * _softplus(abt[:nh, :] + dtt_ref[...])
    g = jnp.dot(ltri, ld, preferred_element_type=F32, precision=lax.Precision.HIGHEST)
    gt = lax.dot_general(ldt, ltri, (((1,), (1,)), ((), ())), preferred_element_type=F32,
                         precision=lax.Precision.HIGHEST)
    gcol_ref[...] = jnp.concatenate([g, jax.nn.sigmoid(ab[:, nh:])], axis=-1)
    grow_ref[...] = gt


def _gdn_gates(hn, w_ab, a_log, dt_bias, *, tm=256):
    t, d = hn.shape
    nh = a_log.shape[0]
    tm = min(tm, t)
    const = lambda shape: pl.BlockSpec(shape, lambda i: (0, 0))
    return pl.pallas_call(
        _gdn_gates_kernel,
        grid=(t // tm,),
        in_specs=[pl.BlockSpec((tm, d), lambda i: (i, 0)),
                  const((d, 2 * nh)), const((2 * nh, d)),
                  const((1, nh)), const((1, nh)), const((nh, 1)), const((nh, 1))],
        out_specs=[pl.BlockSpec((tm, 2 * nh), lambda i: (i, 0)),
                   pl.BlockSpec((nh, tm), lambda i: (0, i))],
        out_shape=[jax.ShapeDtypeStruct((t, 2 * nh), F32),
                   jax.ShapeDtypeStruct((nh, t), F32)],
        compiler_params=_params("parallel"),
        name="gdn_gates",
    )(hn, w_ab, w_ab.T, a_log.reshape(1, nh), dt_bias.reshape(1, nh),
      a_log.reshape(nh, 1), dt_bias.reshape(nh, 1))


def _gdn_kernel(q_ref, k_ref, v_ref, z_ref, gcol_ref, grow_ref, gn_ref, o_ref, state_ref, *, sub):
    n = pl.program_id(1)
    hg = pl.program_id(2)
    tb, wg = q_ref.shape[1], q_ref.shape[2]
    d = GDN_HEAD_DIM
    group = wg // d
    nh = grow_ref.shape[0]
    c = GDN_CHUNK

    @pl.when(n == 0)
    def _():
        for j in range(group):
            state_ref[hg * group + j] = jnp.zeros((d, d), F32)

    q_all = q_ref[0]
    k_all = k_ref[0]
    v_all = v_ref[0]
    z_all = z_ref[0]
    gcol = gcol_ref[0]
    lane = lax.broadcasted_iota(jnp.int32, gcol.shape, 1)

    row = lax.broadcasted_iota(jnp.int32, (sub, sub), 0)
    col = lax.broadcasted_iota(jnp.int32, (sub, sub), 1)
    incl = ((row // c) == (col // c)) & (row >= col)
    diag = row == col
    levels = int(math.log2(c))

    heads_idx = [hg * group + j for j in range(group)]
    states = [state_ref[h] for h in heads_idx]
    nsub = tb // sub
    cps = sub // c

    probs = []
    for j, h in enumerate(heads_idx):
        sl = slice(j * d, (j + 1) * d)
        q, k, v = q_all[:, sl], k_all[:, sl], v_all[:, sl]
        g_col = jnp.sum(jnp.where(lane == h, gcol, 0.0), axis=-1, keepdims=True)
        beta = jnp.sum(jnp.where(lane == nh + h, gcol, 0.0), axis=-1, keepdims=True)
        g_row = grow_ref[pl.ds(h, 1), :]
        for bi in range(nsub):
            rs = slice(bi * sub, (bi + 1) * sub)
            probs.append(dict(q=q[rs], k=k[rs], v=v[rs], beta=beta[rs], g_col=g_col[rs],
                              g_row=g_row[:, rs]))

    for pr in probs:
        pr["decay"] = jnp.where(incl, jnp.exp(jnp.where(incl, pr["g_col"] - pr["g_row"], 0.0)), 0.0)
        pr["eg"] = jnp.exp(pr["g_col"])
        pr["k_beta"] = pr["k"] * pr["beta"]
    for pr in probs:
        pr["kq"] = _dot_nt(jnp.concatenate([pr["k_beta"], pr["q"]], axis=0).astype(BF16),
                           pr["k"].astype(BF16))
    for pr in probs:
        pr["qkb"] = (pr["kq"][sub:] * pr["decay"]).astype(BF16)
        pr["p"] = jnp.where(diag, 0.0, -(pr["kq"][:sub] * pr["decay"]))
        pr["zs"] = jnp.concatenate([pr["v"] * pr["beta"], pr["k_beta"] * pr["eg"]], axis=-1)
    for lev in range(levels):
        for pr in probs:
            pb = pr["p"].astype(BF16)
            pr["zs"] = pr["zs"] + _dot(pb, pr["zs"].astype(BF16))
            if lev + 1 < levels:
                pr["p"] = _dot(pb, pb)
    for pr in probs:
        pr["uw"] = pr["zs"].astype(BF16)
        pr["qk_uw"] = _dot(pr["qkb"], pr["uw"])
    for pr in probs:
        g_col = pr["g_col"]
        g_last = [g_col[(ci + 1) * c - 1:(ci + 1) * c] for ci in range(cps)]
        g_last_col = jnp.concatenate([jnp.broadcast_to(gl, (c, 1)) for gl in g_last], axis=0)
        pr["k_dec"] = (pr["k"] * jnp.exp(g_last_col - g_col)).astype(BF16)
        pr["e_last"] = [jnp.exp(gl) for gl in g_last]
        pr["q_eff"] = (pr["q"] * pr["eg"] - pr["qk_uw"][:, d:]).astype(BF16)
    chunks = [[] for _ in range(group)]
    for ci in range(cps):
        rows = slice(ci * c, (ci + 1) * c)
        for pi, pr in enumerate(probs):
            pr.setdefault("n_p", []).append(_dot_tn(pr["k_dec"][rows], pr["uw"][rows]))
    for pi, pr in enumerate(probs):
        for ci in range(cps):
            rows = slice(ci * c, (ci + 1) * c)
            n_p = pr["n_p"][ci]
            chunks[pi // nsub].append((n_p[:, :d], n_p[:, d:].astype(BF16), pr["q_eff"][rows],
                                       pr["qk_uw"][rows, :d], pr["e_last"][ci]))

    outs = [[] for _ in range(group)]
    for ci in range(tb // c):
        for j in range(group):
            n_c, p_c, q_c, o_c, e_c = chunks[j][ci]
            r = _dot(jnp.concatenate([p_c, q_c], axis=0), states[j].astype(BF16))
            outs[j].append(r[d:] + o_c)
            states[j] = e_c * states[j] - r[:d] + n_c

    ys = []
    for j, h in enumerate(heads_idx):
        state_ref[h] = states[j]
        out = jnp.concatenate(outs[j], axis=0)
        y = out * lax.rsqrt(jnp.mean(out * out, axis=-1, keepdims=True) + NORM_EPS)
        ys.append(y * gn_ref[...] * _silu(z_all[:, j * d:(j + 1) * d]))
    o_ref[0] = jnp.concatenate(ys, axis=-1).astype(o_ref.dtype)


def _gdn(qk, v, zproj, gcol, grow, gdn_norm, *, heads, z_col0, tb=256, group=4, sub=128):
    b, s, _ = qk.shape
    d = GDN_HEAD_DIM
    tb = min(tb, s)
    wg = group * d
    assert heads % group == 0 and z_col0 % wg == 0
    ng = heads // group
    blk = lambda off: pl.BlockSpec((1, tb, wg), lambda bi, ni, hi: (bi, ni, off + hi))
    sub = min(sub, tb)
    assert tb % sub == 0 and sub % GDN_CHUNK == 0
    return pl.pallas_call(
        functools.partial(_gdn_kernel, sub=sub),
        grid=(b, s // tb, ng),
        in_specs=[blk(0), blk(ng), blk(0), blk(z_col0 // wg),
                  pl.BlockSpec((1, tb, 2 * heads), lambda bi, ni, hi: (bi, ni, 0)),
                  pl.BlockSpec((heads, tb), lambda bi, ni, hi: (0, bi * (s // tb) + ni)),
                  pl.BlockSpec((1, d), lambda bi, ni, hi: (0, 0))],
        out_specs=pl.BlockSpec((1, tb, wg), lambda bi, ni, hi: (bi, ni, hi)),
        out_shape=jax.ShapeDtypeStruct((b, s, heads * d), BF16),
        scratch_shapes=[pltpu.VMEM((heads, d, d), F32)],
        compiler_params=_params("arbitrary", "arbitrary", "arbitrary"),
        name="gdn",
    )(qk, qk, v, zproj, gcol, grow, gdn_norm.reshape(1, d))


def _xattn_kernel(q_ref, k_ref, v_ref, o_ref, *, heads):
    d = q_ref.shape[2]
    dh = d // heads
    outs = []
    for h in range(heads):
        sl = slice(h * dh, (h + 1) * dh)
        scores = _dot_nt(q_ref[0, :, sl], k_ref[0, :, sl]) * (dh ** -0.5)
        m = jnp.max(scores, axis=-1, keepdims=True)
        e = jnp.exp(scores - m)
        probs = e / jnp.sum(e, axis=-1, keepdims=True)
        outs.append(_dot(probs.astype(BF16), v_ref[0, :, sl]))
    o_ref[0] = jnp.concatenate(outs, axis=-1).astype(o_ref.dtype)


def _xattn(q, k, v, *, heads, tm=512):
    b, s, d = q.shape
    m = k.shape[1]
    tm = min(tm, s)
    return pl.pallas_call(
        functools.partial(_xattn_kernel, heads=heads),
        grid=(b, s // tm),
        in_specs=[pl.BlockSpec((1, tm, d), lambda bi, i: (bi, i, 0)),
                  pl.BlockSpec((1, m, d), lambda bi, i: (bi, 0, 0)),
                  pl.BlockSpec((1, m, d), lambda bi, i: (bi, 0, 0))],
        out_specs=pl.BlockSpec((1, tm, d), lambda bi, i: (bi, i, 0)),
        out_shape=jax.ShapeDtypeStruct((b, s, d), BF16),
        compiler_params=_params("parallel", "parallel"),
        name="xattn",
    )(q, k, v)


def _token_mixer(hn, positions, w_in, l, ret_norm, gdn_conv, gdn_a_log, gdn_dt_bias, gdn_norm,
                 *, tm=MATMUL_TM, tn=MATMUL_TN):
    b, s = positions.shape
    ret_heads = ret_norm.shape[0]
    gdn_heads = gdn_a_log.shape[0]
    ret_width = ret_heads * RET_HEAD_DIM
    gdn_width = gdn_heads * GDN_HEAD_DIM
    main_width = 4 * ret_width + 4 * gdn_width
    assert (4 * ret_width) % tn == 0 and gdn_width % tn == 0
    t_ret, t_gdn = 4 * ret_width // tn, gdn_width // tn
    proj = _matmul([hn], w_in, l, F32, n_tiles=t_ret + t_gdn, tm=tm, tn=tn,
                   w_tile_of=lambda j: jnp.where(j < t_ret, j, j + 3 * t_gdn), name="in_proj")
    qk = _matmul([hn], w_in, l, F32, n_tiles=2 * t_gdn, w_tile_of=lambda j: t_ret + j, tm=tm, tn=tn,
                 conv=(s, GDN_HEAD_DIM, t_gdn, GDN_HEAD_DIM ** -0.5), conv_w=gdn_conv,
                 name="in_proj_gdn_qk")
    gv = _matmul([hn], w_in, l, F32, n_tiles=t_gdn, w_tile_of=lambda j: t_ret + 2 * t_gdn + j,
                 tm=tm, tn=tn, conv=(s, 0, 0, 1.0), conv_w=gdn_conv, conv_tile0=2 * t_gdn,
                 name="in_proj_gdn_v")
    w_ab = w_in[l, :, main_width:].astype(BF16)
    gcol, grow = _gdn_gates(hn, w_ab, gdn_a_log, gdn_dt_bias)
    proj3 = proj.reshape(b, s, 4 * ret_width + gdn_width)
    ret = _retention(proj3, positions, ret_norm, heads=ret_heads)
    gdn = _gdn(qk.reshape(b, s, 2 * gdn_width), gv.reshape(b, s, gdn_width), proj3,
               gcol.reshape(b, s, 2 * gdn_heads), grow, gdn_norm,
               heads=gdn_heads, z_col0=4 * ret_width)
    return ret.reshape(b * s, ret_width), gdn.reshape(b * s, gdn_width)


def kernel(x, mem, positions, mix_norm, w_in, ret_norm, gdn_conv, gdn_a_log, gdn_dt_bias, gdn_norm,
           w_mix_out, xattn_norm, mem_norm, w_xq, w_xk, w_xv, w_xo, mlp_norm, w_up, w_down, final_norm):
    b, s, d = x.shape
    m = mem.shape[1]
    t = b * s
    depth = w_in.shape[0]

    h = x.reshape(t, d)
    for l in range(depth):
        hn = _rmsnorm(h, mix_norm[l], BF16, name="mix_rmsnorm")
        ret, gdn = _token_mixer(hn, positions, w_in, l, ret_norm[l], gdn_conv[l], gdn_a_log[l],
                                gdn_dt_bias[l], gdn_norm[l])
        h = _matmul([ret, gdn], w_mix_out, l, F32, res=h, name="mix_out")

        hn = _rmsnorm(h, xattn_norm[l], BF16, name="xattn_rmsnorm")
        mn = _rmsnorm(mem.reshape(b * m, d), mem_norm[l], BF16, name="mem_rmsnorm")
        q = _matmul([hn], w_xq, l, BF16, name="xattn_q")
        k = _matmul([mn], w_xk, l, BF16, name="xattn_k")
        v = _matmul([mn], w_xv, l, BF16, name="xattn_v")
        att = _xattn(q.reshape(b, s, d), k.reshape(b, m, d), v.reshape(b, m, d), heads=XATTN_HEADS)
        h = _matmul([att.reshape(t, d)], w_xo, l, F32, res=h, name="xattn_out")

        hn = _rmsnorm(h, mlp_norm[l], BF16, name="mlp_rmsnorm")
        up = _matmul([hn], w_up, l, BF16, relu2=True, name="mlp_up")
        h = _matmul_acc_res(up, w_down[l].astype(BF16), h, name="mlp_down")
    out = _rmsnorm(h, final_norm, x.dtype, name="final_rmsnorm")
    return out.reshape(b, s, d)
```

```python
import functools
import math

import jax
import jax.numpy as jnp
from jax import lax
from jax.experimental import pallas as pl
from jax.experimental.pallas import tpu as pltpu

F32 = jnp.float32
BF16 = jnp.bfloat16

NORM_EPS = 1e-6
ROPE_THETA = 10000.0

RET_HEAD_DIM = 256
GDN_HEAD_DIM = 128
GDN_CHUNK = 64
CONV_WIDTH = 4
XATTN_HEADS = 4
SUBLANES = 8
MATMUL_TM = 1024
MATMUL_TN = 512
CONV_ROW_CHUNK = 256

V7X_VMEM_LIMIT_BYTES = 56 * 1024 * 1024


def _params(*sem):
    return pltpu.CompilerParams(dimension_semantics=sem, vmem_limit_bytes=V7X_VMEM_LIMIT_BYTES)


def _dot(a, b):
    return jnp.dot(a, b, preferred_element_type=F32)


def _dot_nt(a, b):
    return lax.dot_general(a, b, (((1,), (1,)), ((), ())), preferred_element_type=F32)


def _dot_tn(a, b):
    return lax.dot_general(a, b, (((0,), (0,)), ((), ())), preferred_element_type=F32)


def _silu(x):
    return x * jax.nn.sigmoid(x)


def _rmsnorm_kernel(x_ref, g_ref, o_ref):
    x = x_ref[...]
    y = x * lax.rsqrt(jnp.mean(x * x, axis=-1, keepdims=True) + NORM_EPS)
    o_ref[...] = (y * g_ref[...]).astype(o_ref.dtype)


def _rmsnorm(x, gain, out_dtype, tm=512, name="rmsnorm"):
    t, d = x.shape
    tm = min(tm, t)
    return pl.pallas_call(
        _rmsnorm_kernel,
        grid=(t // tm,),
        in_specs=[pl.BlockSpec((tm, d), lambda i: (i, 0)),
                  pl.BlockSpec((1, d), lambda i: (0, 0))],
        out_specs=pl.BlockSpec((tm, d), lambda i: (i, 0)),
        out_shape=jax.ShapeDtypeStruct((t, d), out_dtype),
        compiler_params=_params("parallel"),
        name=name,
    )(x, gain.reshape(1, d))


def _mm_kernel(*refs, n_a, has_res, relu2, conv, w_t):
    a_refs = refs[:n_a]
    w_ref = refs[n_a]
    r_ref = refs[n_a + 1] if has_res else None
    cw_ref = refs[n_a + 1] if conv is not None else None
    if conv is None:
        o_ref, wb_ref = refs[-2], refs[-1]
    else:
        o_ref, wb_ref, xext_ref = refs[-3], refs[-2], refs[-1]
    i = pl.program_id(1)

    @pl.when(i == 0)
    def _():
        wb_ref[...] = w_ref[...].astype(BF16)

    def matmul(rows):
        k0 = 0
        acc = None
        for a_ref in a_refs:
            k1 = k0 + a_ref.shape[1]
            if w_t:
                part = _dot_nt(a_ref[rows, :], wb_ref[:, k0:k1])
            else:
                part = _dot(a_ref[rows, :], wb_ref[k0:k1, :])
            acc = part if acc is None else acc + part
            k0 = k1
        return acc

    tm, tn = o_ref.shape
    if conv is None:
        acc = matmul(slice(0, tm))
        if relu2:
            acc = jnp.square(jnp.maximum(acc, 0.0))
        if has_res:
            acc = r_ref[...] + acc
        o_ref[...] = acc.astype(o_ref.dtype)
        return

    seq_len, l2_head_dim, n_scaled_tiles, scale = conv
    pad = SUBLANES
    rc = min(CONV_ROW_CHUNK, tm)

    @pl.when((i * tm) % seq_len == 0)
    def _():
        xext_ref[:pad, :] = jnp.zeros((pad, tn), F32)

    cw = cw_ref[...]
    mult = jnp.where(pl.program_id(0) < n_scaled_tiles, scale, 1.0)
    for r0 in range(0, tm, rc):
        acc = matmul(slice(r0, r0 + rc))
        xext_ref[pad + r0:pad + r0 + rc, :] = acc
        y = acc * cw[CONV_WIDTH - 1:CONV_WIDTH, :]
        for s in range(1, CONV_WIDTH):
            y = y + xext_ref[pad + r0 - s:pad + r0 - s + rc, :] * cw[CONV_WIDTH - 1 - s:CONV_WIDTH - s, :]
        y = _silu(y)
        if l2_head_dim:
            parts = []
            for h0 in range(0, tn, l2_head_dim):
                yh = y[:, h0:h0 + l2_head_dim]
                parts.append(yh * lax.rsqrt(jnp.sum(yh * yh, axis=-1, keepdims=True) + NORM_EPS) * mult)
            y = jnp.concatenate(parts, axis=-1)
        o_ref[r0:r0 + rc, :] = y.astype(o_ref.dtype)
    xext_ref[:pad, :] = xext_ref[tm:tm + pad, :]


def _matmul(a_list, w, layer, out_dtype, *, n_tiles=None, w_tile_of=lambda j: j, w_t=False, res=None,
            relu2=False, conv=None, conv_w=None, conv_tile0=0, tm=MATMUL_TM, tn=MATMUL_TN, name="matmul"):
    m = a_list[0].shape[0]
    k = sum(a.shape[1] for a in a_list)
    n = w.shape[1] if w_t else w.shape[2]
    assert (w.shape[2] if w_t else w.shape[1]) == k
    tm, tn = min(tm, m), min(tn, n)
    if n_tiles is None:
        assert n % tn == 0
        n_tiles = n // tn
    assert m % tm == 0
    in_specs = [pl.BlockSpec((tm, a.shape[1]), lambda j, i: (i, 0)) for a in a_list]
    if w_t:
        in_specs.append(pl.BlockSpec((None, tn, k), lambda j, i: (layer, w_tile_of(j), 0)))
        scratch = [pltpu.VMEM((tn, k), BF16)]
    else:
        in_specs.append(pl.BlockSpec((None, k, tn), lambda j, i: (layer, 0, w_tile_of(j))))
        scratch = [pltpu.VMEM((k, tn), BF16)]
    args = list(a_list) + [w]
    if res is not None:
        in_specs.append(pl.BlockSpec((tm, tn), lambda j, i: (i, j)))
        args.append(res)
    if conv is not None:
        assert res is None and conv[0] % tm == 0
        in_specs.append(pl.BlockSpec((CONV_WIDTH, tn), lambda j, i: (0, conv_tile0 + j)))
        args.append(conv_w)
        scratch.append(pltpu.VMEM((tm + SUBLANES, tn), F32))
    return pl.pallas_call(
        functools.partial(_mm_kernel, n_a=len(a_list), has_res=res is not None, relu2=relu2, conv=conv,
                          w_t=w_t),
        grid=(n_tiles, m // tm),
        in_specs=in_specs,
        out_specs=pl.BlockSpec((tm, tn), lambda j, i: (i, j)),
        out_shape=jax.ShapeDtypeStruct((m, n_tiles * tn), out_dtype),
        scratch_shapes=scratch,
        compiler_params=_params("parallel", "arbitrary"),
        name=name,
    )(*args)


def _mm_acc_res_kernel(a_ref, w_ref, r_ref, o_ref):
    k = pl.program_id(2)

    @pl.when(k == 0)
    def _():
        o_ref[...] = r_ref[...] + _dot(a_ref[...], w_ref[...])

    @pl.when(k > 0)
    def _():
        o_ref[...] += _dot(a_ref[...], w_ref[...])


def _matmul_acc_res(a, w, res, *, tm=1024, tn=1024, tk=4096, name="matmul_acc_res"):
    m, k = a.shape
    n = w.shape[1]
    tm, tn, tk = min(tm, m), min(tn, n), min(tk, k)
    return pl.pallas_call(
        _mm_acc_res_kernel,
        grid=(m // tm, n // tn, k // tk),
        in_specs=[pl.BlockSpec((tm, tk), lambda i, j, l: (i, l)),
                  pl.BlockSpec((tk, tn), lambda i, j, l: (l, j)),
                  pl.BlockSpec((tm, tn), lambda i, j, l: (i, j))],
        out_specs=pl.BlockSpec((tm, tn), lambda i, j, l: (i, j)),
        out_shape=jax.ShapeDtypeStruct((m, n), F32),
        compiler_params=_params("parallel", "parallel", "arbitrary"),
        name=name,
    )(a, w, res)


def _retention_kernel(lg_ref, pos_ref, freq_ref, q_ref, k_ref, v_ref, g_ref, rn_ref,
                      o_ref, state_ref, dec_ref):
    first = (pl.program_id(0) == 0) & (pl.program_id(1) == 0)
    c = q_ref.shape[1]
    heads, dh = state_ref.shape[0], state_ref.shape[1]
    half = dh // 2
    lg = lg_ref[...]

    @pl.when(first)
    def _():
        row = lax.broadcasted_iota(jnp.int32, (c, c), 0)
        col = lax.broadcasted_iota(jnp.int32, (c, c), 1)
        rel = (row - col).astype(F32)
        for h in range(heads):
            dec_ref[h] = jnp.where(rel >= 0, jnp.exp(jnp.maximum(rel, 0.0) * lg[:, h:h + 1]), 0.0)

    @pl.when(pl.program_id(1) == 0)
    def _():
        state_ref[...] = jnp.zeros(state_ref.shape, F32)

    ang = pos_ref[0].astype(F32) * freq_ref[...]
    cos = jnp.cos(ang)
    sin = jnp.sin(ang)

    def rot(x):
        x1, x2 = x[:, :half], x[:, half:]
        return jnp.concatenate([x1 * cos - x2 * sin, x2 * cos + x1 * sin], axis=-1)

    idx = lax.broadcasted_iota(jnp.int32, (c, 1), 0).astype(F32)
    decay_q = jnp.exp((idx + 1.0) * lg)
    decay_k = jnp.exp((c - 1.0 - idx) * lg)
    decay_chunk = jnp.exp(c * lg)

    hs = range(heads)
    sl = [slice(h * dh, (h + 1) * dh) for h in hs]
    state = [state_ref[h] for h in hs]
    k = [rot(k_ref[0, :, sl[h]]) * (dh ** -0.5) for h in hs]
    qb = [rot(q_ref[0, :, sl[h]]).astype(BF16) for h in hs]
    vb = [v_ref[0, :, sl[h]].astype(BF16) for h in hs]
    scores = [_dot_nt(qb[h], k[h].astype(BF16)) for h in hs]
    inter = [_dot(qb[h], state[h].astype(BF16)) for h in hs]
    kv = [_dot_tn((k[h] * decay_k[:, h:h + 1]).astype(BF16), vb[h]) for h in hs]
    probs = [(scores[h] * dec_ref[h]).astype(BF16) for h in hs]
    intra = [_dot(probs[h], vb[h]) for h in hs]
    for h in hs:
        state_ref[h] = decay_chunk[:, h:h + 1] * state[h] + kv[h]
        out = intra[h] + inter[h] * decay_q[:, h:h + 1]
        y = out * lax.rsqrt(jnp.mean(out * out, axis=-1, keepdims=True) + NORM_EPS)
        y = y * rn_ref[:, sl[h]]
        o_ref[0, :, sl[h]] = (_silu(g_ref[0, :, sl[h]]) * y).astype(o_ref.dtype)


def _retention(proj, positions, ret_norm, *, heads, chunk=256):
    b, s, _ = proj.shape
    dh = RET_HEAD_DIM
    half = dh // 2
    wr = heads * dh
    chunk = min(chunk, s)
    log_gamma = jnp.log1p(-jnp.exp2(-5.0 - jnp.arange(heads, dtype=F32))).reshape(1, heads)
    inv_freq = (ROPE_THETA ** (-jnp.arange(half, dtype=F32) / half)).reshape(1, half)
    blk = lambda off: pl.BlockSpec((1, chunk, wr), lambda bi, ni: (bi, ni, off))
    const = lambda shape: pl.BlockSpec(shape, lambda bi, ni: (0, 0))
    return pl.pallas_call(
        _retention_kernel,
        grid=(b, s // chunk),
        in_specs=[const((1, heads)),
                  pl.BlockSpec((1, chunk, 1), lambda bi, ni: (bi, ni, 0)),
                  const((1, half)),
                  blk(0), blk(1), blk(2), blk(3),
                  const((1, wr))],
        out_specs=pl.BlockSpec((1, chunk, wr), lambda bi, ni: (bi, ni, 0)),
        out_shape=jax.ShapeDtypeStruct((b, s, wr), BF16),
        scratch_shapes=[pltpu.VMEM((heads, dh, dh), F32),
                        pltpu.VMEM((heads, chunk, chunk), F32)],
        compiler_params=_params("arbitrary", "arbitrary"),
        name="retention",
    )(log_gamma, positions.reshape(b, s, 1), inv_freq, proj, proj, proj, proj, ret_norm.reshape(1, wr))


def _block_tril(n, blk, dtype):
    row = lax.broadcasted_iota(jnp.int32, (n, n), 0)
    col = lax.broadcasted_iota(jnp.int32, (n, n), 1)
    return jnp.where((row >= col) & (row // blk == col // blk), 1.0, 0.0).astype(dtype)


def _softplus(x):
    return jnp.maximum(x, 0.0) + jnp.log1p(jnp.exp(-jnp.abs(x)))


def _gdn_gates_kernel(hn_ref, wt_ref, a_ref, dt_ref, at_ref, dtt_ref, gcol_ref, grow_ref):
    nh = a_ref.shape[1]
    tm = hn_ref.shape[0]
    hn = hn_ref[...]
    ab = _dot_nt(hn, wt_ref[...])
    abt = _dot_nt(wt_ref[...], hn)
    ltri = _block_tril(tm, GDN_CHUNK, F32)
    ld = -jnp.exp(a_ref[...]) * _softplus(ab[:, :nh] + dt_ref[...])
    ldt = -jnp.exp(at_ref[...]) * _softplus(abt[:nh, :] + dtt_ref[...])
    g = jnp.dot(ltri, ld, preferred_element_type=F32, precision=lax.Precision.HIGHEST)
    gt = lax.dot_general(ldt, ltri, (((1,), (1,)), ((), ())), preferred_element_type=F32,
                         precision=lax.Precision.HIGHEST)
    gcol_ref[...] = jnp.concatenate([g, jax.nn.sigmoid(ab[:, nh:])], axis=-1)
    grow_ref[...] = gt


def _gdn_gates(hn, w_ab_t, a_log, dt_bias, *, tm=256):
    t, d = hn.shape
    nh = a_log.shape[0]
    tm = min(tm, t)
    const = lambda shape: pl.BlockSpec(shape, lambda i: (0, 0))
    return pl.pallas_call(
        _gdn_gates_kernel,
        grid=(t // tm,),
        in_specs=[pl.BlockSpec((tm, d), lambda i: (i, 0)),
                  const((2 * nh, d)),
                  const((1, nh)), const((1, nh)), const((nh, 1)), const((nh, 1))],
        out_specs=[pl.BlockSpec((tm, 2 * nh), lambda i: (i, 0)),
                   pl.BlockSpec((nh, tm), lambda i: (0, i))],
        out_shape=[jax.ShapeDtypeStruct((t, 2 * nh), F32),
                   jax.ShapeDtypeStruct((nh, t), F32)],
        compiler_params=_params("parallel"),
        name="gdn_gates",
    )(hn, w_ab_t, a_log.reshape(1, nh), dt_bias.reshape(1, nh),
      a_log.reshape(nh, 1), dt_bias.reshape(nh, 1))


def _gdn_kernel(q_ref, k_ref, v_ref, z_ref, gcol_ref, grow_ref, gn_ref, o_ref, state_ref, *, sub):
    n = pl.program_id(1)
    hg = pl.program_id(2)
    tb, wg = q_ref.shape[1], q_ref.shape[2]
    d = GDN_HEAD_DIM
    group = wg // d
    nh = grow_ref.shape[0]
    c = GDN_CHUNK

    @pl.when(n == 0)
    def _():
        for j in range(group):
            state_ref[hg * group + j] = jnp.zeros((d, d), F32)

    q_all = q_ref[0]
    k_all = k_ref[0]
    v_all = v_ref[0]
    z_all = z_ref[0]
    gcol = gcol_ref[0]
    lane = lax.broadcasted_iota(jnp.int32, gcol.shape, 1)

    row = lax.broadcasted_iota(jnp.int32, (sub, sub), 0)
    col = lax.broadcasted_iota(jnp.int32, (sub, sub), 1)
    incl = ((row // c) == (col // c)) & (row >= col)
    diag = row == col
    levels = int(math.log2(c))

    heads_idx = [hg * group + j for j in range(group)]
    states = [state_ref[h] for h in heads_idx]
    nsub = tb // sub
    cps = sub // c

    probs = []
    for j, h in enumerate(heads_idx):
        sl = slice(j * d, (j + 1) * d)
        q, k, v = q_all[:, sl], k_all[:, sl], v_all[:, sl]
        g_col = jnp.sum(jnp.where(lane == h, gcol, 0.0), axis=-1, keepdims=True)
        beta = jnp.sum(jnp.where(lane == nh + h, gcol, 0.0), axis=-1, keepdims=True)
        g_row = grow_ref[pl.ds(h, 1), :]
        for bi in range(nsub):
            rs = slice(bi * sub, (bi + 1) * sub)
            probs.append(dict(q=q[rs], k=k[rs], v=v[rs], beta=beta[rs], g_col=g_col[rs],
                              g_row=g_row[:, rs]))

    for pr in probs:
        pr["decay"] = jnp.where(incl, jnp.exp(jnp.where(incl, pr["g_col"] - pr["g_row"], 0.0)), 0.0)
        pr["eg"] = jnp.exp(pr["g_col"])
        pr["k_beta"] = pr["k"] * pr["beta"]
    for pr in probs:
        pr["kq"] = _dot_nt(jnp.concatenate([pr["k_beta"], pr["q"]], axis=0).astype(BF16),
                           pr["k"].astype(BF16))
    for pr in probs:
        pr["qkb"] = (pr["kq"][sub:] * pr["decay"]).astype(BF16)
        pr["p"] = jnp.where(diag, 0.0, -(pr["kq"][:sub] * pr["decay"]))
        pr["rhs"] = jnp.concatenate([pr["v"] * pr["beta"], pr["k_beta"] * pr["eg"]], axis=-1).astype(BF16)
    for pr in probs:
        pr["y"] = jnp.where(diag, 1.0, pr["p"])
    for lev in range(1, levels):
        for pr in probs:
            pb = pr["p"].astype(BF16)
            if lev == 1:
                pr["p"] = _dot(pb, pb)
                continue
            r = _dot(pb, jnp.concatenate([pb, pr["y"].astype(BF16)], axis=-1))
            pr["p"] = r[:, :sub]
            pr["y"] = pr["y"] + r[:, sub:]
    for pr in probs:
        pr["y"] = pr["y"] + _dot(pr["p"].astype(BF16), pr["y"].astype(BF16))
    for pr in probs:
        pr["uw"] = _dot(pr["y"].astype(BF16), pr["rhs"]).astype(BF16)
        pr["qk_uw"] = _dot(pr["qkb"], pr["uw"])
    for pr in probs:
        g_col = pr["g_col"]
        g_last = [g_col[(ci + 1) * c - 1:(ci + 1) * c] for ci in range(cps)]
        g_last_col = jnp.concatenate([jnp.broadcast_to(gl, (c, 1)) for gl in g_last], axis=0)
        pr["k_dec"] = (pr["k"] * jnp.exp(g_last_col - g_col)).astype(BF16)
        pr["e_last"] = [jnp.exp(gl) for gl in g_last]
        pr["q_eff"] = (pr["q"] * pr["eg"] - pr["qk_uw"][:, d:]).astype(BF16)
    chunks = [[] for _ in range(group)]
    for ci in range(cps):
        rows = slice(ci * c, (ci + 1) * c)
        for pi, pr in enumerate(probs):
            pr.setdefault("n_p", []).append(_dot_tn(pr["k_dec"][rows], pr["uw"][rows]))
    for pi, pr in enumerate(probs):
        for ci in range(cps):
            rows = slice(ci * c, (ci + 1) * c)
            n_p = pr["n_p"][ci]
            chunks[pi // nsub].append((n_p[:, :d], n_p[:, d:].astype(BF16), pr["q_eff"][rows],
                                       pr["qk_uw"][rows, :d], pr["e_last"][ci]))

    outs = [[] for _ in range(group)]
    for ci in range(tb // c):
        for j in range(group):
            n_c, p_c, q_c, o_c, e_c = chunks[j][ci]
            r = _dot(jnp.concatenate([p_c, q_c], axis=0), states[j].astype(BF16))
            outs[j].append(r[d:] + o_c)
            states[j] = e_c * states[j] - r[:d] + n_c

    ys = []
    for j, h in enumerate(heads_idx):
        state_ref[h] = states[j]
        out = jnp.concatenate(outs[j], axis=0)
        y = out * lax.rsqrt(jnp.mean(out * out, axis=-1, keepdims=True) + NORM_EPS)
        ys.append(y * gn_ref[...] * _silu(z_all[:, j * d:(j + 1) * d]))
    o_ref[0] = jnp.concatenate(ys, axis=-1).astype(o_ref.dtype)


def _gdn(qk, v, zproj, gcol, grow, gdn_norm, *, heads, z_col0, tb=256, group=8, sub=128):
    b, s, _ = qk.shape
    d = GDN_HEAD_DIM
    tb = min(tb, s)
    group = min(group, heads)
    wg = group * d
    assert heads % group == 0 and z_col0 % wg == 0
    ng = heads // group
    blk = lambda off: pl.BlockSpec((1, tb, wg), lambda bi, ni, hi: (bi, ni, off + hi))
    sub = min(sub, tb)
    assert tb % sub == 0 and sub % GDN_CHUNK == 0
    return pl.pallas_call(
        functools.partial(_gdn_kernel, sub=sub),
        grid=(b, s // tb, ng),
        in_specs=[blk(0), blk(ng), blk(0), blk(z_col0 // wg),
                  pl.BlockSpec((1, tb, 2 * heads), lambda bi, ni, hi: (bi, ni, 0)),
                  pl.BlockSpec((heads, tb), lambda bi, ni, hi: (0, bi * (s // tb) + ni)),
                  pl.BlockSpec((1, d), lambda bi, ni, hi: (0, 0))],
        out_specs=pl.BlockSpec((1, tb, wg), lambda bi, ni, hi: (bi, ni, hi)),
        out_shape=jax.ShapeDtypeStruct((b, s, heads * d), BF16),
        scratch_shapes=[pltpu.VMEM((heads, d, d), F32)],
        compiler_params=_params("arbitrary", "arbitrary", "arbitrary"),
        name="gdn",
    )(qk, qk, v, zproj, gcol, grow, gdn_norm.reshape(1, d))


def _xattn_kernel(q_ref, k_ref, v_ref, o_ref, *, heads):
    d = q_ref.shape[2]
    dh = d // heads
    outs = []
    for h in range(heads):
        sl = slice(h * dh, (h + 1) * dh)
        scores = _dot_nt(q_ref[0, :, sl], k_ref[0, :, sl]) * (dh ** -0.5)
        m = jnp.max(scores, axis=-1, keepdims=True)
        e = jnp.exp(scores - m)
        probs = e / jnp.sum(e, axis=-1, keepdims=True)
        outs.append(_dot(probs.astype(BF16), v_ref[0, :, sl]))
    o_ref[0] = jnp.concatenate(outs, axis=-1).astype(o_ref.dtype)


def _xattn(q, k, v, *, heads, tm=512):
    b, s, d = q.shape
    m = k.shape[1]
    tm = min(tm, s)
    return pl.pallas_call(
        functools.partial(_xattn_kernel, heads=heads),
        grid=(b, s // tm),
        in_specs=[pl.BlockSpec((1, tm, d), lambda bi, i: (bi, i, 0)),
                  pl.BlockSpec((1, m, d), lambda bi, i: (bi, 0, 0)),
                  pl.BlockSpec((1, m, d), lambda bi, i: (bi, 0, 0))],
        out_specs=pl.BlockSpec((1, tm, d), lambda bi, i: (bi, i, 0)),
        out_shape=jax.ShapeDtypeStruct((b, s, d), BF16),
        compiler_params=_params("parallel", "parallel"),
        name="xattn",
    )(q, k, v)


def _token_mixer(hn, positions, w_in, l, ret_norm, gdn_conv, gdn_a_log, gdn_dt_bias, gdn_norm,
                 *, tm=MATMUL_TM, tn=MATMUL_TN):
    b, s = positions.shape
    ret_heads = ret_norm.shape[0]
    gdn_heads = gdn_a_log.shape[0]
    ret_width = ret_heads * RET_HEAD_DIM
    gdn_width = gdn_heads * GDN_HEAD_DIM
    main_width = 4 * ret_width + 4 * gdn_width
    assert (4 * ret_width) % tn == 0 and gdn_width % tn == 0
    t_ret, t_gdn = 4 * ret_width // tn, gdn_width // tn
    w_in_t = jnp.swapaxes(w_in, 1, 2)
    proj = _matmul([hn], w_in_t, l, F32, n_tiles=t_ret + t_gdn, tm=tm, tn=tn, w_t=True,
                   w_tile_of=lambda j: jnp.where(j < t_ret, j, j + 3 * t_gdn), name="in_proj")
    qk = _matmul([hn], w_in_t, l, F32, n_tiles=2 * t_gdn, w_tile_of=lambda j: t_ret + j, tm=tm, tn=tn,
                 w_t=True, conv=(s, GDN_HEAD_DIM, t_gdn, GDN_HEAD_DIM ** -0.5), conv_w=gdn_conv,
                 name="in_proj_gdn_qk")
    gv = _matmul([hn], w_in_t, l, F32, n_tiles=t_gdn, w_tile_of=lambda j: t_ret + 2 * t_gdn + j,
                 tm=tm, tn=tn, w_t=True, conv=(s, 0, 0, 1.0), conv_w=gdn_conv, conv_tile0=2 * t_gdn,
                 name="in_proj_gdn_v")
    w_ab_t = w_in_t[l, main_width:, :].astype(BF16)
    gcol, grow = _gdn_gates(hn, w_ab_t, gdn_a_log, gdn_dt_bias)
    proj3 = proj.reshape(b, s, 4 * ret_width + gdn_width)
    ret = _retention(proj3, positions, ret_norm, heads=ret_heads)
    gdn = _gdn(qk.reshape(b, s, 2 * gdn_width), gv.reshape(b, s, gdn_width), proj3,
               gcol.reshape(b, s, 2 * gdn_heads), grow, gdn_norm,
               heads=gdn_heads, z_col0=4 * ret_width)
    return ret.reshape(b * s, ret_width), gdn.reshape(b * s, gdn_width)


def kernel(x, mem, positions, mix_norm, w_in, ret_norm, gdn_conv, gdn_a_log, gdn_dt_bias, gdn_norm,
           w_mix_out, xattn_norm, mem_norm, w_xq, w_xk, w_xv, w_xo, mlp_norm, w_up, w_down, final_norm):
    b, s, d = x.shape
    m = mem.shape[1]
    t = b * s
    depth = w_in.shape[0]

    h = x.reshape(t, d)
    for l in range(depth):
        hn = _rmsnorm(h, mix_norm[l], BF16, name="mix_rmsnorm")
        ret, gdn = _token_mixer(hn, positions, w_in, l, ret_norm[l], gdn_conv[l], gdn_a_log[l],
                                gdn_dt_bias[l], gdn_norm[l])
        h = _matmul([ret, gdn], w_mix_out, l, F32, res=h, name="mix_out")

        hn = _rmsnorm(h, xattn_norm[l], BF16, name="xattn_rmsnorm")
        mn = _rmsnorm(mem.reshape(b * m, d), mem_norm[l], BF16, name="mem_rmsnorm")
        q = _matmul([hn], w_xq, l, BF16, name="xattn_q")
        k = _matmul([mn], w_xk, l, BF16, name="xattn_k")
        v = _matmul([mn], w_xv, l, BF16, name="xattn_v")
        att = _xattn(q.reshape(b, s, d), k.reshape(b, m, d), v.reshape(b, m, d), heads=XATTN_HEADS)
        h = _matmul([att.reshape(t, d)], w_xo, l, F32, res=h, name="xattn_out")

        hn = _rmsnorm(h, mlp_norm[l], BF16, name="mlp_rmsnorm")
        up = _matmul([hn], w_up, l, BF16, relu2=True, name="mlp_up")
        h = _matmul_acc_res(up, w_down[l].astype(BF16), h, name="mlp_down")
    out = _rmsnorm(h, final_norm, x.dtype, name="final_rmsnorm")
    return out.reshape(b, s, d)
```

```python
import functools
import math

import jax
import jax.numpy as jnp
from jax import lax
from jax.experimental import pallas as pl
from jax.experimental.pallas import tpu as pltpu

F32 = jnp.float32
BF16 = jnp.bfloat16

NORM_EPS = 1e-6
ROPE_THETA = 10000.0

RET_HEAD_DIM = 256
GDN_HEAD_DIM = 128
GDN_CHUNK = 64
CONV_WIDTH = 4
XATTN_HEADS = 4
SUBLANES = 8
MATMUL_TM = 1024
MATMUL_TN = 512
MATMUL_TN_WIDE = 1024
CONV_ROW_CHUNK = 256

V7X_VMEM_LIMIT_BYTES = 56 * 1024 * 1024


def _params(*sem):
    return pltpu.CompilerParams(dimension_semantics=sem, vmem_limit_bytes=V7X_VMEM_LIMIT_BYTES)


def _dot(a, b):
    return jnp.dot(a, b, preferred_element_type=F32)


def _dot_nt(a, b):
    return lax.dot_general(a, b, (((1,), (1,)), ((), ())), preferred_element_type=F32)


def _dot_tn(a, b):
    return lax.dot_general(a, b, (((0,), (0,)), ((), ())), preferred_element_type=F32)


def _silu(x):
    return x * jax.nn.sigmoid(x)


def _rmsnorm_kernel(x_ref, g_ref, o_ref):
    x = x_ref[...]
    y = x * lax.rsqrt(jnp.mean(x * x, axis=-1, keepdims=True) + NORM_EPS)
    o_ref[...] = (y * g_ref[...]).astype(o_ref.dtype)


def _rmsnorm(x, gain, out_dtype, tm=512, name="rmsnorm"):
    t, d = x.shape
    tm = min(tm, t)
    return pl.pallas_call(
        _rmsnorm_kernel,
        grid=(t // tm,),
        in_specs=[pl.BlockSpec((tm, d), lambda i: (i, 0)),
                  pl.BlockSpec((1, d), lambda i: (0, 0))],
        out_specs=pl.BlockSpec((tm, d), lambda i: (i, 0)),
        out_shape=jax.ShapeDtypeStruct((t, d), out_dtype),
        compiler_params=_params("parallel"),
        name=name,
    )(x, gain.reshape(1, d))


def _mm_kernel(*refs, n_a, has_res, relu2, conv, w_t, prefetch):
    a_refs = refs[:n_a]
    w_ref = refs[n_a]
    r_ref = refs[n_a + 1] if has_res else None
    cw_ref = refs[n_a + 1] if conv is not None else None
    n_in = n_a + 1 + (1 if has_res or conv is not None else 0)
    o_ref, wb_ref = refs[n_in], refs[n_in + 1]
    scratch = list(refs[n_in + 2:])
    xext_ref = scratch.pop(0) if conv is not None else None
    j = pl.program_id(0)
    i = pl.program_id(1)
    nj = pl.num_programs(0)

    if prefetch is None:
        @pl.when(i == 0)
        def _():
            wb_ref[...] = w_ref[...].astype(BF16)
    else:
        layer, w_tile_of = prefetch
        wf_ref, sem = scratch
        blk = wf_ref.shape[0] if w_t else wf_ref.shape[1]

        def fetch(jj):
            start = pl.multiple_of(w_tile_of(jj) * blk, blk)
            src = w_ref.at[layer, pl.ds(start, blk), :] if w_t else w_ref.at[layer, :, pl.ds(start, blk)]
            return pltpu.make_async_copy(src, wf_ref, sem)

        @pl.when(i == 0)
        def _():
            @pl.when(j == 0)
            def _():
                fetch(j).start()

            fetch(j).wait()
            wb_ref[...] = wf_ref[...].astype(BF16)

            @pl.when(j + 1 < nj)
            def _():
                fetch(j + 1).start()

    def matmul(rows):
        k0 = 0
        acc = None
        for a_ref in a_refs:
            k1 = k0 + a_ref.shape[1]
            if w_t:
                part = _dot_nt(a_ref[rows, :], wb_ref[:, k0:k1])
            else:
                part = _dot(a_ref[rows, :], wb_ref[k0:k1, :])
            acc = part if acc is None else acc + part
            k0 = k1
        return acc

    tm, tn = o_ref.shape
    if conv is None:
        acc = matmul(slice(0, tm))
        if relu2:
            acc = jnp.square(jnp.maximum(acc, 0.0))
        if has_res:
            acc = r_ref[...] + acc
        o_ref[...] = acc.astype(o_ref.dtype)
        return

    seq_len, l2_head_dim, n_scaled_tiles, scale = conv
    pad = SUBLANES
    rc = min(CONV_ROW_CHUNK, tm)

    @pl.when((i * tm) % seq_len == 0)
    def _():
        xext_ref[:pad, :] = jnp.zeros((pad, tn), F32)

    cw = cw_ref[...]
    mult = jnp.where(pl.program_id(0) < n_scaled_tiles, scale, 1.0)
    for r0 in range(0, tm, rc):
        acc = matmul(slice(r0, r0 + rc))
        xext_ref[pad + r0:pad + r0 + rc, :] = acc
        y = acc * cw[CONV_WIDTH - 1:CONV_WIDTH, :]
        for s in range(1, CONV_WIDTH):
            y = y + xext_ref[pad + r0 - s:pad + r0 - s + rc, :] * cw[CONV_WIDTH - 1 - s:CONV_WIDTH - s, :]
        y = _silu(y)
        if l2_head_dim:
            parts = []
            for h0 in range(0, tn, l2_head_dim):
                yh = y[:, h0:h0 + l2_head_dim]
                parts.append(yh * lax.rsqrt(jnp.sum(yh * yh, axis=-1, keepdims=True) + NORM_EPS) * mult)
            y = jnp.concatenate(parts, axis=-1)
        o_ref[r0:r0 + rc, :] = y.astype(o_ref.dtype)
    xext_ref[:pad, :] = xext_ref[tm:tm + pad, :]


def _matmul(a_list, w, layer, out_dtype, *, n_tiles=None, w_tile_of=lambda j: j, w_t=False, res=None,
            relu2=False, conv=None, conv_w=None, conv_tile0=0, prefetch=False, tm=MATMUL_TM, tn=MATMUL_TN,
            name="matmul"):
    m = a_list[0].shape[0]
    k = sum(a.shape[1] for a in a_list)
    n = w.shape[1] if w_t else w.shape[2]
    assert (w.shape[2] if w_t else w.shape[1]) == k
    tm, tn = min(tm, m), min(tn, n)
    if n_tiles is None:
        assert n % tn == 0
        n_tiles = n // tn
    assert m % tm == 0
    in_specs = [pl.BlockSpec((tm, a.shape[1]), lambda j, i: (i, 0)) for a in a_list]
    w_block = (tn, k) if w_t else (k, tn)
    if prefetch:
        in_specs.append(pl.BlockSpec(memory_space=pl.ANY))
    elif w_t:
        in_specs.append(pl.BlockSpec((None, tn, k), lambda j, i: (layer, w_tile_of(j), 0)))
    else:
        in_specs.append(pl.BlockSpec((None, k, tn), lambda j, i: (layer, 0, w_tile_of(j))))
    scratch = [pltpu.VMEM(w_block, BF16)]
    args = list(a_list) + [w]
    if res is not None:
        in_specs.append(pl.BlockSpec((tm, tn), lambda j, i: (i, j)))
        args.append(res)
    if conv is not None:
        assert res is None and conv[0] % tm == 0
        in_specs.append(pl.BlockSpec((CONV_WIDTH, tn), lambda j, i: (0, conv_tile0 + j)))
        args.append(conv_w)
        scratch.append(pltpu.VMEM((tm + SUBLANES, tn), F32))
    if prefetch:
        scratch += [pltpu.VMEM(w_block, F32), pltpu.SemaphoreType.DMA(())]
    return pl.pallas_call(
        functools.partial(_mm_kernel, n_a=len(a_list), has_res=res is not None, relu2=relu2, conv=conv,
                          w_t=w_t, prefetch=(layer, w_tile_of) if prefetch else None),
        grid=(n_tiles, m // tm),
        in_specs=in_specs,
        out_specs=pl.BlockSpec((tm, tn), lambda j, i: (i, j)),
        out_shape=jax.ShapeDtypeStruct((m, n_tiles * tn), out_dtype),
        scratch_shapes=scratch,
        compiler_params=_params("arbitrary" if prefetch else "parallel", "arbitrary"),
        name=name,
    )(*args)


def _mm_acc_res_kernel(a_ref, w_ref, r_ref, o_ref):
    k = pl.program_id(2)

    @pl.when(k == 0)
    def _():
        o_ref[...] = r_ref[...] + _dot(a_ref[...], w_ref[...])

    @pl.when(k > 0)
    def _():
        o_ref[...] += _dot(a_ref[...], w_ref[...])


def _matmul_acc_res(a, w, res, *, tm=1024, tn=1024, tk=4096, name="matmul_acc_res"):
    m, k = a.shape
    n = w.shape[1]
    tm, tn, tk = min(tm, m), min(tn, n), min(tk, k)
    return pl.pallas_call(
        _mm_acc_res_kernel,
        grid=(m // tm, n // tn, k // tk),
        in_specs=[pl.BlockSpec((tm, tk), lambda i, j, l: (i, l)),
                  pl.BlockSpec((tk, tn), lambda i, j, l: (l, j)),
                  pl.BlockSpec((tm, tn), lambda i, j, l: (i, j))],
        out_specs=pl.BlockSpec((tm, tn), lambda i, j, l: (i, j)),
        out_shape=jax.ShapeDtypeStruct((m, n), F32),
        compiler_params=_params("parallel", "parallel", "arbitrary"),
        name=name,
    )(a, w, res)


def _retention_kernel(lg_ref, pos_ref, freq_ref, q_ref, k_ref, v_ref, g_ref, rn_ref,
                      o_ref, state_ref, dec_ref):
    first = (pl.program_id(0) == 0) & (pl.program_id(1) == 0)
    c = q_ref.shape[1]
    heads, dh = state_ref.shape[0], state_ref.shape[1]
    half = dh // 2
    lg = lg_ref[...]

    @pl.when(first)
    def _():
        row = lax.broadcasted_iota(jnp.int32, (c, c), 0)
        col = lax.broadcasted_iota(jnp.int32, (c, c), 1)
        rel = (row - col).astype(F32)
        for h in range(heads):
            dec_ref[h] = jnp.where(rel >= 0, jnp.exp(jnp.maximum(rel, 0.0) * lg[:, h:h + 1]), 0.0)

    @pl.when(pl.program_id(1) == 0)
    def _():
        state_ref[...] = jnp.zeros(state_ref.shape, F32)

    ang = pos_ref[0].astype(F32) * freq_ref[...]
    cos = jnp.cos(ang)
    sin = jnp.sin(ang)

    def rot(x):
        x1, x2 = x[:, :half], x[:, half:]
        return jnp.concatenate([x1 * cos - x2 * sin, x2 * cos + x1 * sin], axis=-1)

    idx = lax.broadcasted_iota(jnp.int32, (c, 1), 0).astype(F32)
    decay_q = jnp.exp((idx + 1.0) * lg)
    decay_k = jnp.exp((c - 1.0 - idx) * lg)
    decay_chunk = jnp.exp(c * lg)

    hs = range(heads)
    sl = [slice(h * dh, (h + 1) * dh) for h in hs]
    state = [state_ref[h] for h in hs]
    k = [rot(k_ref[0, :, sl[h]]) * (dh ** -0.5) for h in hs]
    qb = [rot(q_ref[0, :, sl[h]]).astype(BF16) for h in hs]
    vb = [v_ref[0, :, sl[h]].astype(BF16) for h in hs]
    scores = [_dot_nt(qb[h], k[h].astype(BF16)) for h in hs]
    inter = [_dot(qb[h], state[h].astype(BF16)) for h in hs]
    kv = [_dot_tn((k[h] * decay_k[:, h:h + 1]).astype(BF16), vb[h]) for h in hs]
    probs = [(scores[h] * dec_ref[h]).astype(BF16) for h in hs]
    intra = [_dot(probs[h], vb[h]) for h in hs]
    for h in hs:
        state_ref[h] = decay_chunk[:, h:h + 1] * state[h] + kv[h]
        out = intra[h] + inter[h] * decay_q[:, h:h + 1]
        y = out * lax.rsqrt(jnp.mean(out * out, axis=-1, keepdims=True) + NORM_EPS)
        y = y * rn_ref[:, sl[h]]
        o_ref[0, :, sl[h]] = (_silu(g_ref[0, :, sl[h]]) * y).astype(o_ref.dtype)


def _retention(proj, positions, ret_norm, *, heads, chunk=256):
    b, s, _ = proj.shape
    dh = RET_HEAD_DIM
    half = dh // 2
    wr = heads * dh
    chunk = min(chunk, s)
    log_gamma = jnp.log1p(-jnp.exp2(-5.0 - jnp.arange(heads, dtype=F32))).reshape(1, heads)
    inv_freq = (ROPE_THETA ** (-jnp.arange(half, dtype=F32) / half)).reshape(1, half)
    blk = lambda off: pl.BlockSpec((1, chunk, wr), lambda bi, ni: (bi, ni, off))
    const = lambda shape: pl.BlockSpec(shape, lambda bi, ni: (0, 0))
    return pl.pallas_call(
        _retention_kernel,
        grid=(b, s // chunk),
        in_specs=[const((1, heads)),
                  pl.BlockSpec((1, chunk, 1), lambda bi, ni: (bi, ni, 0)),
                  const((1, half)),
                  blk(0), blk(1), blk(2), blk(3),
                  const((1, wr))],
        out_specs=pl.BlockSpec((1, chunk, wr), lambda bi, ni: (bi, ni, 0)),
        out_shape=jax.ShapeDtypeStruct((b, s, wr), BF16),
        scratch_shapes=[pltpu.VMEM((heads, dh, dh), F32),
                        pltpu.VMEM((heads, chunk, chunk), F32)],
        compiler_params=_params("arbitrary", "arbitrary"),
        name="retention",
    )(log_gamma, positions.reshape(b, s, 1), inv_freq, proj, proj, proj, proj, ret_norm.reshape(1, wr))


def _block_tril(n, blk, dtype):
    row = lax.broadcasted_iota(jnp.int32, (n, n), 0)
    col = lax.broadcasted_iota(jnp.int32, (n, n), 1)
    return jnp.where((row >= col) & (row // blk == col // blk), 1.0, 0.0).astype(dtype)


def _bf16_pieces(x):
    hi = x.astype(BF16)
    r1 = x - hi.astype(F32)
    mid = r1.astype(BF16)
    lo = (r1 - mid.astype(F32)).astype(BF16)
    return hi, mid, lo


def _softplus(x):
    return jnp.maximum(x, 0.0) + jnp.log1p(jnp.exp(-jnp.abs(x)))


def _gdn_gates_kernel(hn_ref, wt_ref, a_ref, dt_ref, at_ref, dtt_ref, gcol_ref, grow_ref):
    nh = a_ref.shape[1]
    tm = hn_ref.shape[0]
    hn = hn_ref[...]
    ab = _dot_nt(hn, wt_ref[...])
    abt = _dot_nt(wt_ref[...], hn)
    ltri = _block_tril(tm, GDN_CHUNK, BF16)
    ld = -jnp.exp(a_ref[...]) * _softplus(ab[:, :nh] + dt_ref[...])
    ldt = -jnp.exp(at_ref[...]) * _softplus(abt[:nh, :] + dtt_ref[...])
    g = sum(_dot(ltri, piece) for piece in _bf16_pieces(ld))
    gt = sum(_dot_nt(piece, ltri) for piece in _bf16_pieces(ldt))
    gcol_ref[...] = jnp.concatenate([g, jax.nn.sigmoid(ab[:, nh:])], axis=-1)
    grow_ref[...] = gt


def _gdn_gates(hn, w_ab_t, a_log, dt_bias, *, tm=256):
    t, d = hn.shape
    nh = a_log.shape[0]
    tm = min(tm, t)
    const = lambda shape: pl.BlockSpec(shape, lambda i: (0, 0))
    return pl.pallas_call(
        _gdn_gates_kernel,
        grid=(t // tm,),
        in_specs=[pl.BlockSpec((tm, d), lambda i: (i, 0)),
                  const((2 * nh, d)),
                  const((1, nh)), const((1, nh)), const((nh, 1)), const((nh, 1))],
        out_specs=[pl.BlockSpec((tm, 2 * nh), lambda i: (i, 0)),
                   pl.BlockSpec((nh, tm), lambda i: (0, i))],
        out_shape=[jax.ShapeDtypeStruct((t, 2 * nh), F32),
                   jax.ShapeDtypeStruct((nh, t), F32)],
        compiler_params=_params("parallel"),
        name="gdn_gates",
    )(hn, w_ab_t, a_log.reshape(1, nh), dt_bias.reshape(1, nh),
      a_log.reshape(nh, 1), dt_bias.reshape(nh, 1))


def _gdn_kernel(q_ref, k_ref, v_ref, z_ref, gcol_ref, grow_ref, gn_ref, o_ref, state_ref, *, sub):
    n = pl.program_id(1)
    hg = pl.program_id(2)
    tb, wg = q_ref.shape[1], q_ref.shape[2]
    d = GDN_HEAD_DIM
    group = wg // d
    nh = grow_ref.shape[0]
    c = GDN_CHUNK

    @pl.when(n == 0)
    def _():
        for j in range(group):
            state_ref[hg * group + j] = jnp.zeros((d, d), F32)

    q_all = q_ref[0]
    k_all = k_ref[0]
    v_all = v_ref[0]
    z_all = z_ref[0]
    gcol = gcol_ref[0]
    lane = lax.broadcasted_iota(jnp.int32, gcol.shape, 1)

    row = lax.broadcasted_iota(jnp.int32, (sub, sub), 0)
    col = lax.broadcasted_iota(jnp.int32, (sub, sub), 1)
    incl = ((row // c) == (col // c)) & (row >= col)
    diag = row == col
    levels = int(math.log2(c))

    heads_idx = [hg * group + j for j in range(group)]
    states = [state_ref[h] for h in heads_idx]
    nsub = tb // sub
    cps = sub // c

    probs = []
    for j, h in enumerate(heads_idx):
        sl = slice(j * d, (j + 1) * d)
        q, k, v = q_all[:, sl], k_all[:, sl], v_all[:, sl]
        g_col = jnp.sum(jnp.where(lane == h, gcol, 0.0), axis=-1, keepdims=True)
        beta = jnp.sum(jnp.where(lane == nh + h, gcol, 0.0), axis=-1, keepdims=True)
        g_row = grow_ref[pl.ds(h, 1), :]
        for bi in range(nsub):
            rs = slice(bi * sub, (bi + 1) * sub)
            probs.append(dict(q=q[rs], k=k[rs], v=v[rs], beta=beta[rs], g_col=g_col[rs],
                              g_row=g_row[:, rs]))

    for pr in probs:
        pr["decay"] = jnp.where(incl, jnp.exp(jnp.where(incl, pr["g_col"] - pr["g_row"], 0.0)), 0.0)
        pr["eg"] = jnp.exp(pr["g_col"])
        pr["k_beta"] = pr["k"] * pr["beta"]
    for pr in probs:
        pr["kq"] = _dot_nt(jnp.concatenate([pr["k_beta"], pr["q"]], axis=0).astype(BF16),
                           pr["k"].astype(BF16))
    for pr in probs:
        pr["qkb"] = (pr["kq"][sub:] * pr["decay"]).astype(BF16)
        pr["p"] = jnp.where(diag, 0.0, -(pr["kq"][:sub] * pr["decay"]))
        pr["rhs"] = jnp.concatenate([pr["v"] * pr["beta"], pr["k_beta"] * pr["eg"]], axis=-1).astype(BF16)
    for pr in probs:
        pr["y"] = jnp.where(diag, 1.0, pr["p"])
    for lev in range(1, levels):
        for pr in probs:
            pb = pr["p"].astype(BF16)
            if lev == 1:
                pr["p"] = _dot(pb, pb)
                continue
            r = _dot(pb, jnp.concatenate([pb, pr["y"].astype(BF16)], axis=-1))
            pr["p"] = r[:, :sub]
            pr["y"] = pr["y"] + r[:, sub:]
    for pr in probs:
        pr["y"] = pr["y"] + _dot(pr["p"].astype(BF16), pr["y"].astype(BF16))
    for pr in probs:
        pr["uw"] = _dot(pr["y"].astype(BF16), pr["rhs"]).astype(BF16)
        pr["qk_uw"] = _dot(pr["qkb"], pr["uw"])
    for pr in probs:
        g_col = pr["g_col"]
        g_last = [g_col[(ci + 1) * c - 1:(ci + 1) * c] for ci in range(cps)]
        g_last_col = jnp.concatenate([jnp.broadcast_to(gl, (c, 1)) for gl in g_last], axis=0)
        pr["k_dec"] = (pr["k"] * jnp.exp(g_last_col - g_col)).astype(BF16)
        pr["e_last"] = [jnp.exp(gl) for gl in g_last]
        pr["q_eff"] = (pr["q"] * pr["eg"] - pr["qk_uw"][:, d:]).astype(BF16)
    chunks = [[] for _ in range(group)]
    for ci in range(cps):
        rows = slice(ci * c, (ci + 1) * c)
        for pi, pr in enumerate(probs):
            pr.setdefault("n_p", []).append(_dot_tn(pr["k_dec"][rows], pr["uw"][rows]))
    for pi, pr in enumerate(probs):
        for ci in range(cps):
            rows = slice(ci * c, (ci + 1) * c)
            n_p = pr["n_p"][ci]
            chunks[pi // nsub].append((n_p[:, :d], n_p[:, d:].astype(BF16), pr["q_eff"][rows],
                                       pr["qk_uw"][rows, :d], pr["e_last"][ci]))

    outs = [[] for _ in range(group)]
    for ci in range(tb // c):
        for j in range(group):
            n_c, p_c, q_c, o_c, e_c = chunks[j][ci]
            r = _dot(jnp.concatenate([p_c, q_c], axis=0), states[j].astype(BF16))
            outs[j].append(r[d:] + o_c)
            states[j] = e_c * states[j] - r[:d] + n_c

    ys = []
    for j, h in enumerate(heads_idx):
        state_ref[h] = states[j]
        out = jnp.concatenate(outs[j], axis=0)
        y = out * lax.rsqrt(jnp.mean(out * out, axis=-1, keepdims=True) + NORM_EPS)
        ys.append(y * gn_ref[...] * _silu(z_all[:, j * d:(j + 1) * d]))
    o_ref[0] = jnp.concatenate(ys, axis=-1).astype(o_ref.dtype)


def _gdn(qk, v, zproj, gcol, grow, gdn_norm, *, heads, z_col0, tb=256, group=8, sub=128):
    b, s, _ = qk.shape
    d = GDN_HEAD_DIM
    tb = min(tb, s)
    group = min(group, heads)
    wg = group * d
    assert heads % group == 0 and z_col0 % wg == 0
    ng = heads // group
    blk = lambda off: pl.BlockSpec((1, tb, wg), lambda bi, ni, hi: (bi, ni, off + hi))
    sub = min(sub, tb)
    assert tb % sub == 0 and sub % GDN_CHUNK == 0
    return pl.pallas_call(
        functools.partial(_gdn_kernel, sub=sub),
        grid=(b, s // tb, ng),
        in_specs=[blk(0), blk(ng), blk(0), blk(z_col0 // wg),
                  pl.BlockSpec((1, tb, 2 * heads), lambda bi, ni, hi: (bi, ni, 0)),
                  pl.BlockSpec((heads, tb), lambda bi, ni, hi: (0, bi * (s // tb) + ni)),
                  pl.BlockSpec((1, d), lambda bi, ni, hi: (0, 0))],
        out_specs=pl.BlockSpec((1, tb, wg), lambda bi, ni, hi: (bi, ni, hi)),
        out_shape=jax.ShapeDtypeStruct((b, s, heads * d), BF16),
        scratch_shapes=[pltpu.VMEM((heads, d, d), F32)],
        compiler_params=_params("arbitrary", "arbitrary", "arbitrary"),
        name="gdn",
    )(qk, qk, v, zproj, gcol, grow, gdn_norm.reshape(1, d))


def _xattn_kernel(q_ref, k_ref, v_ref, o_ref, *, heads):
    d = q_ref.shape[2]
    dh = d // heads
    outs = []
    for h in range(heads):
        sl = slice(h * dh, (h + 1) * dh)
        scores = _dot_nt(q_ref[0, :, sl], k_ref[0, :, sl]) * (dh ** -0.5)
        m = jnp.max(scores, axis=-1, keepdims=True)
        e = jnp.exp(scores - m)
        probs = e / jnp.sum(e, axis=-1, keepdims=True)
        outs.append(_dot(probs.astype(BF16), v_ref[0, :, sl]))
    o_ref[0] = jnp.concatenate(outs, axis=-1).astype(o_ref.dtype)


def _xattn(q, k, v, *, heads, tm=512):
    b, s, d = q.shape
    m = k.shape[1]
    tm = min(tm, s)
    return pl.pallas_call(
        functools.partial(_xattn_kernel, heads=heads),
        grid=(b, s // tm),
        in_specs=[pl.BlockSpec((1, tm, d), lambda bi, i: (bi, i, 0)),
                  pl.BlockSpec((1, m, d), lambda bi, i: (bi, 0, 0)),
                  pl.BlockSpec((1, m, d), lambda bi, i: (bi, 0, 0))],
        out_specs=pl.BlockSpec((1, tm, d), lambda bi, i: (bi, i, 0)),
        out_shape=jax.ShapeDtypeStruct((b, s, d), BF16),
        compiler_params=_params("parallel", "parallel"),
        name="xattn",
    )(q, k, v)


def _token_mixer(hn, positions, w_in, l, ret_norm, gdn_conv, gdn_a_log, gdn_dt_bias, gdn_norm,
                 *, tm=MATMUL_TM, tn=MATMUL_TN_WIDE):
    b, s = positions.shape
    ret_heads = ret_norm.shape[0]
    gdn_heads = gdn_a_log.shape[0]
    ret_width = ret_heads * RET_HEAD_DIM
    gdn_width = gdn_heads * GDN_HEAD_DIM
    main_width = 4 * ret_width + 4 * gdn_width
    assert (4 * ret_width) % tn == 0 and gdn_width % tn == 0
    t_ret, t_gdn = 4 * ret_width // tn, gdn_width // tn
    w_in_t = jnp.swapaxes(w_in, 1, 2)
    proj = _matmul([hn], w_in_t, l, F32, n_tiles=t_ret + t_gdn, tm=tm, tn=tn, w_t=True, prefetch=True,
                   w_tile_of=lambda j: jnp.where(j < t_ret, j, j + 3 * t_gdn), name="in_proj")
    qk = _matmul([hn], w_in_t, l, F32, n_tiles=2 * t_gdn, w_tile_of=lambda j: t_ret + j, tm=tm, tn=tn,
                 w_t=True, prefetch=True, conv=(s, GDN_HEAD_DIM, t_gdn, GDN_HEAD_DIM ** -0.5),
                 conv_w=gdn_conv, name="in_proj_gdn_qk")
    gv = _matmul([hn], w_in_t, l, F32, n_tiles=t_gdn, w_tile_of=lambda j: t_ret + 2 * t_gdn + j,
                 tm=tm, tn=tn, w_t=True, prefetch=True, conv=(s, 0, 0, 1.0), conv_w=gdn_conv,
                 conv_tile0=2 * t_gdn, name="in_proj_gdn_v")
    w_ab_t = w_in_t[l, main_width:, :].astype(BF16)
    gcol, grow = _gdn_gates(hn, w_ab_t, gdn_a_log, gdn_dt_bias)
    proj3 = proj.reshape(b, s, 4 * ret_width + gdn_width)
    ret = _retention(proj3, positions, ret_norm, heads=ret_heads)
    gdn = _gdn(qk.reshape(b, s, 2 * gdn_width), gv.reshape(b, s, gdn_width), proj3,
               gcol.reshape(b, s, 2 * gdn_heads), grow, gdn_norm,
               heads=gdn_heads, z_col0=4 * ret_width)
    return ret.reshape(b * s, ret_width), gdn.reshape(b * s, gdn_width)


def kernel(x, mem, positions, mix_norm, w_in, ret_norm, gdn_conv, gdn_a_log, gdn_dt_bias, gdn_norm,
           w_mix_out, xattn_norm, mem_norm, w_xq, w_xk, w_xv, w_xo, mlp_norm, w_up, w_down, final_norm):
    b, s, d = x.shape
    m = mem.shape[1]
    t = b * s
    depth = w_in.shape[0]

    h = x.reshape(t, d)
    for l in range(depth):
        hn = _rmsnorm(h, mix_norm[l], BF16, name="mix_rmsnorm")
        ret, gdn = _token_mixer(hn, positions, w_in, l, ret_norm[l], gdn_conv[l], gdn_a_log[l],
                                gdn_dt_bias[l], gdn_norm[l])
        h = _matmul([ret, gdn], w_mix_out, l, F32, res=h, name="mix_out")

        hn = _rmsnorm(h, xattn_norm[l], BF16, name="xattn_rmsnorm")
        mn = _rmsnorm(mem.reshape(b * m, d), mem_norm[l], BF16, name="mem_rmsnorm")
        q = _matmul([hn], w_xq, l, BF16, tn=MATMUL_TN_WIDE, prefetch=True, name="xattn_q")
        k = _matmul([mn], w_xk, l, BF16, name="xattn_k")
        v = _matmul([mn], w_xv, l, BF16, name="xattn_v")
        att = _xattn(q.reshape(b, s, d), k.reshape(b, m, d), v.reshape(b, m, d), heads=XATTN_HEADS)
        h = _matmul([att.reshape(t, d)], w_xo, l, F32, res=h, name="xattn_out")

        hn = _rmsnorm(h, mlp_norm[l], BF16, name="mlp_rmsnorm")
        up = _matmul([hn], w_up, l, BF16, relu2=True, tn=MATMUL_TN_WIDE, prefetch=True, name="mlp_up")
        h = _matmul_acc_res(up, w_down[l].astype(BF16), h, name="mlp_down")
    out = _rmsnorm(h, final_norm, x.dtype, name="final_rmsnorm")
    return out.reshape(b, s, d)
```

```python
import functools
import math

import jax
import jax.numpy as jnp
from jax import lax
from jax.experimental import pallas as pl
from jax.experimental.pallas import tpu as pltpu

F32 = jnp.float32
BF16 = jnp.bfloat16

NORM_EPS = 1e-6
ROPE_THETA = 10000.0

RET_HEAD_DIM = 256
GDN_HEAD_DIM = 128
GDN_CHUNK = 64
CONV_WIDTH = 4
XATTN_HEADS = 4
SUBLANES = 8
MATMUL_TM = 1024
MATMUL_TM_RES = 512
MATMUL_TN = 512
MATMUL_TN_WIDE = 1024
CONV_ROW_CHUNK = 256

V7X_VMEM_LIMIT_BYTES = 56 * 1024 * 1024


def _params(*sem):
    return pltpu.CompilerParams(dimension_semantics=sem, vmem_limit_bytes=V7X_VMEM_LIMIT_BYTES)


def _dot(a, b):
    return jnp.dot(a, b, preferred_element_type=F32)


def _dot_nt(a, b):
    return lax.dot_general(a, b, (((1,), (1,)), ((), ())), preferred_element_type=F32)


def _dot_tn(a, b):
    return lax.dot_general(a, b, (((0,), (0,)), ((), ())), preferred_element_type=F32)


def _silu(x):
    return x * jax.nn.sigmoid(x)


def _rmsnorm_kernel(x_ref, g_ref, o_ref):
    x = x_ref[...]
    y = x * lax.rsqrt(jnp.mean(x * x, axis=-1, keepdims=True) + NORM_EPS)
    o_ref[...] = (y * g_ref[...]).astype(o_ref.dtype)


def _rmsnorm(x, gain, out_dtype, tm=512, name="rmsnorm"):
    t, d = x.shape
    tm = min(tm, t)
    return pl.pallas_call(
        _rmsnorm_kernel,
        grid=(t // tm,),
        in_specs=[pl.BlockSpec((tm, d), lambda i: (i, 0)),
                  pl.BlockSpec((1, d), lambda i: (0, 0))],
        out_specs=pl.BlockSpec((tm, d), lambda i: (i, 0)),
        out_shape=jax.ShapeDtypeStruct((t, d), out_dtype),
        compiler_params=_params("parallel"),
        name=name,
    )(x, gain.reshape(1, d))


def _mm_kernel(*refs, n_a, has_res, relu2, conv, w_t, prefetch, has_side):
    a_refs = refs[:n_a]
    w_ref = refs[n_a]
    r_ref = refs[n_a + 1] if has_res else None
    cw_ref = refs[n_a + 1] if conv is not None else None
    n_in = n_a + 1 + (1 if has_res or conv is not None else 0)
    if has_side:
        side_in_ref, side_out_ref = refs[n_in], refs[n_in + 2]
        side_out_ref[...] = side_in_ref[...].astype(BF16)
        refs = refs[:n_in] + (refs[n_in + 1],) + refs[n_in + 3:]
    o_ref, wb_ref = refs[n_in], refs[n_in + 1]
    scratch = list(refs[n_in + 2:])
    xext_ref = scratch.pop(0) if conv is not None else None
    j = pl.program_id(0)
    i = pl.program_id(1)
    nj = pl.num_programs(0)

    if prefetch is None:
        @pl.when(i == 0)
        def _():
            wb_ref[...] = w_ref[...].astype(BF16)
    else:
        layer, w_tile_of = prefetch
        wf_ref, sem = scratch
        blk = wf_ref.shape[0] if w_t else wf_ref.shape[1]

        def fetch(jj):
            start = pl.multiple_of(w_tile_of(jj) * blk, blk)
            src = w_ref.at[layer, pl.ds(start, blk), :] if w_t else w_ref.at[layer, :, pl.ds(start, blk)]
            return pltpu.make_async_copy(src, wf_ref, sem)

        @pl.when(i == 0)
        def _():
            @pl.when(j == 0)
            def _():
                fetch(j).start()

            fetch(j).wait()
            wb_ref[...] = wf_ref[...].astype(BF16)

            @pl.when(j + 1 < nj)
            def _():
                fetch(j + 1).start()

    def matmul(rows):
        k0 = 0
        acc = None
        for a_ref in a_refs:
            k1 = k0 + a_ref.shape[1]
            if w_t:
                part = _dot_nt(a_ref[rows, :], wb_ref[:, k0:k1])
            else:
                part = _dot(a_ref[rows, :], wb_ref[k0:k1, :])
            acc = part if acc is None else acc + part
            k0 = k1
        return acc

    tm, tn = o_ref.shape
    if conv is None:
        acc = matmul(slice(0, tm))
        if relu2:
            acc = jnp.square(jnp.maximum(acc, 0.0))
        if has_res:
            acc = r_ref[...] + acc
        o_ref[...] = acc.astype(o_ref.dtype)
        return

    seq_len, l2_head_dim, n_scaled_tiles, scale = conv
    pad = SUBLANES
    rc = min(CONV_ROW_CHUNK, tm)

    @pl.when((i * tm) % seq_len == 0)
    def _():
        xext_ref[:pad, :] = jnp.zeros((pad, tn), F32)

    cw = cw_ref[...]
    mult = jnp.where(pl.program_id(0) < n_scaled_tiles, scale, 1.0)
    for r0 in range(0, tm, rc):
        acc = matmul(slice(r0, r0 + rc))
        xext_ref[pad + r0:pad + r0 + rc, :] = acc
        y = acc * cw[CONV_WIDTH - 1:CONV_WIDTH, :]
        for s in range(1, CONV_WIDTH):
            y = y + xext_ref[pad + r0 - s:pad + r0 - s + rc, :] * cw[CONV_WIDTH - 1 - s:CONV_WIDTH - s, :]
        y = _silu(y)
        if l2_head_dim:
            parts = []
            for h0 in range(0, tn, l2_head_dim):
                yh = y[:, h0:h0 + l2_head_dim]
                parts.append(yh * (lax.rsqrt(jnp.sum(yh * yh, axis=-1, keepdims=True) + NORM_EPS) * mult))
            y = jnp.concatenate(parts, axis=-1)
        o_ref[r0:r0 + rc, :] = y.astype(o_ref.dtype)
    xext_ref[:pad, :] = xext_ref[tm:tm + pad, :]


def _matmul(a_list, w, layer, out_dtype, *, n_tiles=None, w_tile_of=lambda j: j, w_t=False, res=None,
            relu2=False, conv=None, conv_w=None, conv_tile0=0, prefetch=False, side=None, tm=MATMUL_TM,
            tn=MATMUL_TN, name="matmul"):
    m = a_list[0].shape[0]
    k = sum(a.shape[1] for a in a_list)
    n = w.shape[1] if w_t else w.shape[2]
    assert (w.shape[2] if w_t else w.shape[1]) == k
    tm, tn = min(tm, m), min(tn, n)
    if n_tiles is None:
        assert n % tn == 0
        n_tiles = n // tn
    assert m % tm == 0
    in_specs = [pl.BlockSpec((tm, a.shape[1]), lambda j, i: (i, 0)) for a in a_list]
    w_block = (tn, k) if w_t else (k, tn)
    if prefetch:
        in_specs.append(pl.BlockSpec(memory_space=pl.ANY))
    elif w_t:
        in_specs.append(pl.BlockSpec((None, tn, k), lambda j, i: (layer, w_tile_of(j), 0)))
    else:
        in_specs.append(pl.BlockSpec((None, k, tn), lambda j, i: (layer, 0, w_tile_of(j))))
    scratch = [pltpu.VMEM(w_block, BF16)]
    args = list(a_list) + [w]
    if res is not None:
        in_specs.append(pl.BlockSpec((tm, tn), lambda j, i: (i, j)))
        args.append(res)
    if conv is not None:
        assert res is None and conv[0] % tm == 0
        in_specs.append(pl.BlockSpec((CONV_WIDTH, tn), lambda j, i: (0, conv_tile0 + j)))
        args.append(conv_w)
        scratch.append(pltpu.VMEM((tm + SUBLANES, tn), F32))
    if prefetch:
        scratch += [pltpu.VMEM(w_block, F32), pltpu.SemaphoreType.DMA(())]
    nm = m // tm
    out_specs = pl.BlockSpec((tm, tn), lambda j, i: (i, j))
    out_shape = jax.ShapeDtypeStruct((m, n_tiles * tn), out_dtype)
    if side is not None:
        rows, cols = side.shape
        steps = n_tiles * nm
        assert rows % steps == 0 and (rows // steps) % 16 == 0
        side_spec = pl.BlockSpec((rows // steps, cols), lambda j, i: (j * nm + i, 0))
        in_specs.append(side_spec)
        args.append(side)
        out_specs = [out_specs, side_spec]
        out_shape = [out_shape, jax.ShapeDtypeStruct((rows, cols), BF16)]
    return pl.pallas_call(
        functools.partial(_mm_kernel, n_a=len(a_list), has_res=res is not None, relu2=relu2, conv=conv,
                          w_t=w_t, prefetch=(layer, w_tile_of) if prefetch else None,
                          has_side=side is not None),
        grid=(n_tiles, nm),
        in_specs=in_specs,
        out_specs=out_specs,
        out_shape=out_shape,
        scratch_shapes=scratch,
        compiler_params=_params("arbitrary" if prefetch else "parallel", "arbitrary"),
        name=name,
    )(*args)


def _mm_acc_res_kernel(a_ref, w_ref, r_ref, o_ref):
    k = pl.program_id(2)

    @pl.when(k == 0)
    def _():
        o_ref[...] = r_ref[...] + _dot(a_ref[...], w_ref[...])

    @pl.when(k > 0)
    def _():
        o_ref[...] += _dot(a_ref[...], w_ref[...])


def _matmul_acc_res(a, w, res, *, tm=1024, tn=1024, tk=4096, name="matmul_acc_res"):
    m, k = a.shape
    n = w.shape[1]
    tm, tn, tk = min(tm, m), min(tn, n), min(tk, k)
    return pl.pallas_call(
        _mm_acc_res_kernel,
        grid=(m // tm, n // tn, k // tk),
        in_specs=[pl.BlockSpec((tm, tk), lambda i, j, l: (i, l)),
                  pl.BlockSpec((tk, tn), lambda i, j, l: (l, j)),
                  pl.BlockSpec((tm, tn), lambda i, j, l: (i, j))],
        out_specs=pl.BlockSpec((tm, tn), lambda i, j, l: (i, j)),
        out_shape=jax.ShapeDtypeStruct((m, n), F32),
        compiler_params=_params("parallel", "parallel", "arbitrary"),
        name=name,
    )(a, w, res)


def _retention_kernel(lg_ref, pos_ref, freq_ref, q_ref, k_ref, v_ref, g_ref, rn_ref,
                      o_ref, state_ref, dec_ref):
    first = (pl.program_id(0) == 0) & (pl.program_id(1) == 0)
    c = q_ref.shape[1]
    heads, dh = state_ref.shape[0], state_ref.shape[1]
    half = dh // 2
    lg = lg_ref[...]

    @pl.when(first)
    def _():
        row = lax.broadcasted_iota(jnp.int32, (c, c), 0)
        col = lax.broadcasted_iota(jnp.int32, (c, c), 1)
        rel = (row - col).astype(F32)
        for h in range(heads):
            dec_ref[h] = jnp.where(rel >= 0, jnp.exp(jnp.maximum(rel, 0.0) * lg[:, h:h + 1]), 0.0)

    @pl.when(pl.program_id(1) == 0)
    def _():
        state_ref[...] = jnp.zeros(state_ref.shape, F32)

    ang = pos_ref[0].astype(F32) * freq_ref[...]
    cos = jnp.cos(ang)
    sin = jnp.sin(ang)

    def rot(x):
        x1, x2 = x[:, :half], x[:, half:]
        return jnp.concatenate([x1 * cos - x2 * sin, x2 * cos + x1 * sin], axis=-1)

    idx = lax.broadcasted_iota(jnp.int32, (c, 1), 0).astype(F32)
    decay_q = jnp.exp((idx + 1.0) * lg)
    decay_k = jnp.exp((c - 1.0 - idx) * lg)
    decay_chunk = jnp.exp(c * lg)

    hs = range(heads)
    sl = [slice(h * dh, (h + 1) * dh) for h in hs]
    state = [state_ref[h] for h in hs]
    k = [rot(k_ref[0, :, sl[h]]) * (dh ** -0.5) for h in hs]
    qb = [rot(q_ref[0, :, sl[h]]).astype(BF16) for h in hs]
    vb = [v_ref[0, :, sl[h]].astype(BF16) for h in hs]
    scores = [_dot_nt(qb[h], k[h].astype(BF16)) for h in hs]
    inter = [_dot(qb[h], state[h].astype(BF16)) for h in hs]
    kv = [_dot_tn((k[h] * decay_k[:, h:h + 1]).astype(BF16), vb[h]) for h in hs]
    probs = [(scores[h] * dec_ref[h]).astype(BF16) for h in hs]
    intra = [_dot(probs[h], vb[h]) for h in hs]
    for h in hs:
        state_ref[h] = decay_chunk[:, h:h + 1] * state[h] + kv[h]
        out = intra[h] + inter[h] * decay_q[:, h:h + 1]
        y = out * lax.rsqrt(jnp.mean(out * out, axis=-1, keepdims=True) + NORM_EPS)
        y = y * rn_ref[:, sl[h]]
        o_ref[0, :, sl[h]] = (_silu(g_ref[0, :, sl[h]]) * y).astype(o_ref.dtype)


def _retention(proj, positions, ret_norm, *, heads, chunk=256):
    b, s, _ = proj.shape
    dh = RET_HEAD_DIM
    half = dh // 2
    wr = heads * dh
    chunk = min(chunk, s)
    log_gamma = jnp.log1p(-jnp.exp2(-5.0 - jnp.arange(heads, dtype=F32))).reshape(1, heads)
    inv_freq = (ROPE_THETA ** (-jnp.arange(half, dtype=F32) / half)).reshape(1, half)
    blk = lambda off: pl.BlockSpec((1, chunk, wr), lambda bi, ni: (bi, ni, off))
    const = lambda shape: pl.BlockSpec(shape, lambda bi, ni: (0, 0))
    return pl.pallas_call(
        _retention_kernel,
        grid=(b, s // chunk),
        in_specs=[const((1, heads)),
                  pl.BlockSpec((1, chunk, 1), lambda bi, ni: (bi, ni, 0)),
                  const((1, half)),
                  blk(0), blk(1), blk(2), blk(3),
                  const((1, wr))],
        out_specs=pl.BlockSpec((1, chunk, wr), lambda bi, ni: (bi, ni, 0)),
        out_shape=jax.ShapeDtypeStruct((b, s, wr), BF16),
        scratch_shapes=[pltpu.VMEM((heads, dh, dh), F32),
                        pltpu.VMEM((heads, chunk, chunk), F32)],
        compiler_params=_params("arbitrary", "arbitrary"),
        name="retention",
    )(log_gamma, positions.reshape(b, s, 1), inv_freq, proj, proj, proj, proj, ret_norm.reshape(1, wr))


def _block_tril(n, blk, dtype):
    row = lax.broadcasted_iota(jnp.int32, (n, n), 0)
    col = lax.broadcasted_iota(jnp.int32, (n, n), 1)
    return jnp.where((row >= col) & (row // blk == col // blk), 1.0, 0.0).astype(dtype)


def _bf16_pieces(x):
    hi = x.astype(BF16)
    r1 = x - hi.astype(F32)
    mid = r1.astype(BF16)
    lo = (r1 - mid.astype(F32)).astype(BF16)
    return hi, mid, lo


def _softplus(x):
    return jnp.maximum(x, 0.0) + jnp.log1p(jnp.exp(-jnp.abs(x)))


def _gdn_gates_kernel(hn_ref, wt_ref, a_ref, dt_ref, at_ref, dtt_ref, gcol_ref, grow_ref):
    nh = a_ref.shape[1]
    tm = hn_ref.shape[0]
    hn = hn_ref[...]
    ab = _dot_nt(hn, wt_ref[...])
    abt = _dot_nt(wt_ref[...], hn)
    ltri = _block_tril(tm, GDN_CHUNK, BF16)
    ld = -jnp.exp(a_ref[...]) * _softplus(ab[:, :nh] + dt_ref[...])
    ldt = -jnp.exp(at_ref[...]) * _softplus(abt[:nh, :] + dtt_ref[...])
    g = sum(_dot(ltri, piece) for piece in _bf16_pieces(ld))
    gt = sum(_dot_nt(piece, ltri) for piece in _bf16_pieces(ldt))
    gcol_ref[...] = jnp.concatenate([g, jax.nn.sigmoid(ab[:, nh:])], axis=-1)
    grow_ref[...] = gt


def _gdn_gates(hn, w_ab_t, a_log, dt_bias, *, tm=256):
    t, d = hn.shape
    nh = a_log.shape[0]
    tm = min(tm, t)
    const = lambda shape: pl.BlockSpec(shape, lambda i: (0, 0))
    return pl.pallas_call(
        _gdn_gates_kernel,
        grid=(t // tm,),
        in_specs=[pl.BlockSpec((tm, d), lambda i: (i, 0)),
                  const((2 * nh, d)),
                  const((1, nh)), const((1, nh)), const((nh, 1)), const((nh, 1))],
        out_specs=[pl.BlockSpec((tm, 2 * nh), lambda i: (i, 0)),
                   pl.BlockSpec((nh, tm), lambda i: (0, i))],
        out_shape=[jax.ShapeDtypeStruct((t, 2 * nh), F32),
                   jax.ShapeDtypeStruct((nh, t), F32)],
        compiler_params=_params("parallel"),
        name="gdn_gates",
    )(hn, w_ab_t, a_log.reshape(1, nh), dt_bias.reshape(1, nh),
      a_log.reshape(nh, 1), dt_bias.reshape(nh, 1))


def _gdn_kernel(q_ref, k_ref, v_ref, z_ref, gcol_ref, grow_ref, gn_ref, o_ref, state_ref, *, sub):
    n = pl.program_id(1)
    hg = pl.program_id(2)
    tb, wg = q_ref.shape[1], q_ref.shape[2]
    d = GDN_HEAD_DIM
    group = wg // d
    nh = grow_ref.shape[0]
    c = GDN_CHUNK

    @pl.when(n == 0)
    def _():
        for j in range(group):
            state_ref[hg * group + j] = jnp.zeros((d, d), F32)

    q_all = q_ref[0]
    k_all = k_ref[0]
    v_all = v_ref[0]
    z_all = z_ref[0]
    gcol = gcol_ref[0]
    lane = lax.broadcasted_iota(jnp.int32, gcol.shape, 1)

    row = lax.broadcasted_iota(jnp.int32, (sub, sub), 0)
    col = lax.broadcasted_iota(jnp.int32, (sub, sub), 1)
    incl = ((row // c) == (col // c)) & (row >= col)
    diag = row == col
    levels = int(math.log2(c))

    heads_idx = [hg * group + j for j in range(group)]
    states = [state_ref[h] for h in heads_idx]
    nsub = tb // sub
    cps = sub // c

    probs = []
    for j, h in enumerate(heads_idx):
        sl = slice(j * d, (j + 1) * d)
        q, k, v = q_all[:, sl], k_all[:, sl], v_all[:, sl]
        g_col = jnp.sum(jnp.where(lane == h, gcol, 0.0), axis=-1, keepdims=True)
        beta = jnp.sum(jnp.where(lane == nh + h, gcol, 0.0), axis=-1, keepdims=True)
        g_row = grow_ref[pl.ds(h, 1), :]
        for bi in range(nsub):
            rs = slice(bi * sub, (bi + 1) * sub)
            probs.append(dict(q=q[rs], k=k[rs], v=v[rs], beta=beta[rs], g_col=g_col[rs],
                              g_row=g_row[:, rs]))

    for pr in probs:
        pr["decay"] = jnp.where(incl, jnp.exp(jnp.where(incl, pr["g_col"] - pr["g_row"], 0.0)), 0.0)
        pr["eg"] = jnp.exp(pr["g_col"])
        pr["k_beta"] = pr["k"] * pr["beta"]
    for pr in probs:
        pr["kq"] = _dot_nt(jnp.concatenate([pr["k_beta"], pr["q"]], axis=0).astype(BF16),
                           pr["k"].astype(BF16))
    for pr in probs:
        pr["qkb"] = (pr["kq"][sub:] * pr["decay"]).astype(BF16)
        pr["p"] = jnp.where(diag, 0.0, -(pr["kq"][:sub] * pr["decay"]))
        pr["rhs"] = jnp.concatenate([pr["v"] * pr["beta"], pr["k_beta"] * pr["eg"]], axis=-1).astype(BF16)
    for pr in probs:
        pr["y"] = jnp.where(diag, 1.0, pr["p"])
    for lev in range(1, levels):
        for pr in probs:
            pb = pr["p"].astype(BF16)
            if lev == 1:
                pr["p"] = _dot(pb, pb)
                continue
            r = _dot(pb, jnp.concatenate([pb, pr["y"].astype(BF16)], axis=-1))
            pr["p"] = r[:, :sub]
            pr["y"] = pr["y"] + r[:, sub:]
    for pr in probs:
        pr["y"] = pr["y"] + _dot(pr["p"].astype(BF16), pr["y"].astype(BF16))
    for pr in probs:
        pr["uw"] = _dot(pr["y"].astype(BF16), pr["rhs"]).astype(BF16)
        pr["qk_uw"] = _dot(pr["qkb"], pr["uw"])
    for pr in probs:
        g_col = pr["g_col"]
        g_last = [g_col[(ci + 1) * c - 1:(ci + 1) * c] for ci in range(cps)]
        g_last_col = jnp.concatenate([jnp.broadcast_to(gl, (c, 1)) for gl in g_last], axis=0)
        pr["k_dec"] = (pr["k"] * jnp.exp(g_last_col - g_col)).astype(BF16)
        pr["e_last"] = [jnp.exp(gl) for gl in g_last]
        pr["q_eff"] = (pr["q"] * pr["eg"] - pr["qk_uw"][:, d:]).astype(BF16)
    chunks = [[] for _ in range(group)]
    for ci in range(cps):
        rows = slice(ci * c, (ci + 1) * c)
        for pi, pr in enumerate(probs):
            pr.setdefault("n_p", []).append(_dot_tn(pr["k_dec"][rows], pr["uw"][rows]))
    for pi, pr in enumerate(probs):
        for ci in range(cps):
            rows = slice(ci * c, (ci + 1) * c)
            n_p = pr["n_p"][ci]
            chunks[pi // nsub].append((n_p[:, :d], n_p[:, d:].astype(BF16), pr["q_eff"][rows],
                                       pr["qk_uw"][rows, :d], pr["e_last"][ci]))

    outs = [[] for _ in range(group)]
    for ci in range(tb // c):
        for j in range(group):
            n_c, p_c, q_c, o_c, e_c = chunks[j][ci]
            r = _dot(jnp.concatenate([p_c, q_c], axis=0), states[j].astype(BF16))
            outs[j].append(r[d:] + o_c)
            states[j] = e_c * states[j] - r[:d] + n_c

    ys = []
    for j, h in enumerate(heads_idx):
        state_ref[h] = states[j]
        out = jnp.concatenate(outs[j], axis=0)
        y = out * lax.rsqrt(jnp.mean(out * out, axis=-1, keepdims=True) + NORM_EPS)
        ys.append(y * gn_ref[...] * _silu(z_all[:, j * d:(j + 1) * d]))
    o_ref[0] = jnp.concatenate(ys, axis=-1).astype(o_ref.dtype)


def _gdn(qk, v, zproj, gcol, grow, gdn_norm, *, heads, z_col0, tb=256, group=8, sub=128):
    b, s, _ = qk.shape
    d = GDN_HEAD_DIM
    tb = min(tb, s)
    group = min(group, heads)
    wg = group * d
    assert heads % group == 0 and z_col0 % wg == 0
    ng = heads // group
    blk = lambda off: pl.BlockSpec((1, tb, wg), lambda bi, ni, hi: (bi, ni, off + hi))
    sub = min(sub, tb)
    assert tb % sub == 0 and sub % GDN_CHUNK == 0
    return pl.pallas_call(
        functools.partial(_gdn_kernel, sub=sub),
        grid=(b, s // tb, ng),
        in_specs=[blk(0), blk(ng), blk(0), blk(z_col0 // wg),
                  pl.BlockSpec((1, tb, 2 * heads), lambda bi, ni, hi: (bi, ni, 0)),
                  pl.BlockSpec((heads, tb), lambda bi, ni, hi: (0, bi * (s // tb) + ni)),
                  pl.BlockSpec((1, d), lambda bi, ni, hi: (0, 0))],
        out_specs=pl.BlockSpec((1, tb, wg), lambda bi, ni, hi: (bi, ni, hi)),
        out_shape=jax.ShapeDtypeStruct((b, s, heads * d), BF16),
        scratch_shapes=[pltpu.VMEM((heads, d, d), F32)],
        compiler_params=_params("arbitrary", "arbitrary", "arbitrary"),
        name="gdn",
    )(qk, qk, v, zproj, gcol, grow, gdn_norm.reshape(1, d))


def _xattn_kernel(q_ref, k_ref, v_ref, o_ref, *, heads):
    d = q_ref.shape[2]
    dh = d // heads
    sl = [slice(h * dh, (h + 1) * dh) for h in range(heads)]
    scores = [_dot_nt(q_ref[0, :, s], k_ref[0, :, s]) * (dh ** -0.5) for s in sl]
    probs = []
    for sc in scores:
        e = jnp.exp(sc - jnp.max(sc, axis=-1, keepdims=True))
        probs.append((e / jnp.sum(e, axis=-1, keepdims=True)).astype(BF16))
    for s, p in zip(sl, probs):
        o_ref[0, :, s] = _dot(p, v_ref[0, :, s]).astype(o_ref.dtype)


def _xattn(q, k, v, *, heads, tm=512):
    b, s, d = q.shape
    m = k.shape[1]
    tm = min(tm, s)
    return pl.pallas_call(
        functools.partial(_xattn_kernel, heads=heads),
        grid=(b, s // tm),
        in_specs=[pl.BlockSpec((1, tm, d), lambda bi, i: (bi, i, 0)),
                  pl.BlockSpec((1, m, d), lambda bi, i: (bi, 0, 0)),
                  pl.BlockSpec((1, m, d), lambda bi, i: (bi, 0, 0))],
        out_specs=pl.BlockSpec((1, tm, d), lambda bi, i: (bi, i, 0)),
        out_shape=jax.ShapeDtypeStruct((b, s, d), BF16),
        compiler_params=_params("parallel", "parallel"),
        name="xattn",
    )(q, k, v)


def _token_mixer(hn, positions, w_in, l, ret_norm, gdn_conv, gdn_a_log, gdn_dt_bias, gdn_norm,
                 *, tm=MATMUL_TM, tn=MATMUL_TN_WIDE):
    b, s = positions.shape
    ret_heads = ret_norm.shape[0]
    gdn_heads = gdn_a_log.shape[0]
    ret_width = ret_heads * RET_HEAD_DIM
    gdn_width = gdn_heads * GDN_HEAD_DIM
    main_width = 4 * ret_width + 4 * gdn_width
    assert (4 * ret_width) % tn == 0 and gdn_width % tn == 0
    t_ret, t_gdn = 4 * ret_width // tn, gdn_width // tn
    w_in_t = jnp.swapaxes(w_in, 1, 2)
    proj = _matmul([hn], w_in_t, l, F32, n_tiles=t_ret + t_gdn, tm=tm, tn=tn, w_t=True, prefetch=True,
                   w_tile_of=lambda j: jnp.where(j < t_ret, j, j + 3 * t_gdn), name="in_proj")
    qk = _matmul([hn], w_in_t, l, F32, n_tiles=2 * t_gdn, w_tile_of=lambda j: t_ret + j, tm=tm, tn=tn,
                 w_t=True, prefetch=True, conv=(s, GDN_HEAD_DIM, t_gdn, GDN_HEAD_DIM ** -0.5),
                 conv_w=gdn_conv, name="in_proj_gdn_qk")
    gv = _matmul([hn], w_in_t, l, F32, n_tiles=t_gdn, w_tile_of=lambda j: t_ret + 2 * t_gdn + j,
                 tm=tm, tn=tn, w_t=True, prefetch=True, conv=(s, 0, 0, 1.0), conv_w=gdn_conv,
                 conv_tile0=2 * t_gdn, name="in_proj_gdn_v")
    w_ab_t = w_in_t[l, main_width:, :].astype(BF16)
    gcol, grow = _gdn_gates(hn, w_ab_t, gdn_a_log, gdn_dt_bias)
    proj3 = proj.reshape(b, s, 4 * ret_width + gdn_width)
    ret = _retention(proj3, positions, ret_norm, heads=ret_heads)
    gdn = _gdn(qk.reshape(b, s, 2 * gdn_width), gv.reshape(b, s, gdn_width), proj3,
               gcol.reshape(b, s, 2 * gdn_heads), grow, gdn_norm,
               heads=gdn_heads, z_col0=4 * ret_width)
    return ret.reshape(b * s, ret_width), gdn.reshape(b * s, gdn_width)


def kernel(x, mem, positions, mix_norm, w_in, ret_norm, gdn_conv, gdn_a_log, gdn_dt_bias, gdn_norm,
           w_mix_out, xattn_norm, mem_norm, w_xq, w_xk, w_xv, w_xo, mlp_norm, w_up, w_down, final_norm):
    b, s, d = x.shape
    m = mem.shape[1]
    t = b * s
    depth = w_in.shape[0]

    h = x.reshape(t, d)
    for l in range(depth):
        hn = _rmsnorm(h, mix_norm[l], BF16, name="mix_rmsnorm")
        ret, gdn = _token_mixer(hn, positions, w_in, l, ret_norm[l], gdn_conv[l], gdn_a_log[l],
                                gdn_dt_bias[l], gdn_norm[l])
        h = _matmul([ret, gdn], w_mix_out, l, F32, res=h, tm=MATMUL_TM_RES, tn=MATMUL_TN_WIDE,
                    prefetch=True, name="mix_out")

        hn = _rmsnorm(h, xattn_norm[l], BF16, name="xattn_rmsnorm")
        mn = _rmsnorm(mem.reshape(b * m, d), mem_norm[l], BF16, name="mem_rmsnorm")
        q = _matmul([hn], w_xq, l, BF16, tn=MATMUL_TN_WIDE, prefetch=True, name="xattn_q")
        k = _matmul([mn], w_xk, l, BF16, name="xattn_k")
        v = _matmul([mn], w_xv, l, BF16, name="xattn_v")
        att = _xattn(q.reshape(b, s, d), k.reshape(b, m, d), v.reshape(b, m, d), heads=XATTN_HEADS)
        h = _matmul([att.reshape(t, d)], w_xo, l, F32, res=h, tm=MATMUL_TM_RES, tn=MATMUL_TN_WIDE,
                    prefetch=True, name="xattn_out")

        hn = _rmsnorm(h, mlp_norm[l], BF16, name="mlp_rmsnorm")
        up, w_down_bf16 = _matmul([hn], w_up, l, BF16, relu2=True, tn=MATMUL_TN_WIDE, prefetch=True,
                                  side=w_down[l], name="mlp_up")
        h = _matmul_acc_res(up, w_down_bf16, h, name="mlp_down")
    out = _rmsnorm(h, final_norm, x.dtype, name="final_rmsnorm")
    return out.reshape(b, s, d)
```

```python
import functools
import math

import jax
import jax.numpy as jnp
from jax import lax
from jax.experimental import pallas as pl
from jax.experimental.pallas import tpu as pltpu

F32 = jnp.float32
BF16 = jnp.bfloat16

NORM_EPS = 1e-6
ROPE_THETA = 10000.0

RET_HEAD_DIM = 256
GDN_HEAD_DIM = 128
GDN_CHUNK = 64
CONV_WIDTH = 4
XATTN_HEADS = 4
SUBLANES = 8
LANES = 128
MATMUL_TM = 1024
MATMUL_TM_RES = 512
MATMUL_TN = 512
MATMUL_TN_WIDE = 1024
CONV_ROW_CHUNK = 256

V7X_VMEM_LIMIT_BYTES = 58 * 1024 * 1024


def _params(*sem):
    return pltpu.CompilerParams(dimension_semantics=sem, vmem_limit_bytes=V7X_VMEM_LIMIT_BYTES)


def _dot(a, b):
    return jnp.dot(a, b, preferred_element_type=F32)


def _dot_nt(a, b):
    return lax.dot_general(a, b, (((1,), (1,)), ((), ())), preferred_element_type=F32)


def _dot_tn(a, b):
    return lax.dot_general(a, b, (((0,), (0,)), ((), ())), preferred_element_type=F32)


def _silu(x):
    return x * jax.nn.sigmoid(x)


def _rmsnorm_kernel(x_ref, g_ref, o_ref):
    x = x_ref[...]
    y = x * lax.rsqrt(jnp.mean(x * x, axis=-1, keepdims=True) + NORM_EPS)
    o_ref[...] = (y * g_ref[...]).astype(o_ref.dtype)


def _rmsnorm(x, gain, out_dtype, tm=512, name="rmsnorm"):
    t, d = x.shape
    tm = min(tm, t)
    return pl.pallas_call(
        _rmsnorm_kernel,
        grid=(t // tm,),
        in_specs=[pl.BlockSpec((tm, d), lambda i: (i, 0)),
                  pl.BlockSpec((1, d), lambda i: (0, 0))],
        out_specs=pl.BlockSpec((tm, d), lambda i: (i, 0)),
        out_shape=jax.ShapeDtypeStruct((t, d), out_dtype),
        compiler_params=_params("parallel"),
        name=name,
    )(x, gain.reshape(1, d))


def _mm_kernel(*refs, n_a, has_res, relu2, conv, w_t, prefetch, has_side, norm_out, norm_in):
    refs = list(refs)
    take = lambda present: refs.pop(0) if present else None
    a_refs = [refs.pop(0) for _ in range(n_a)]
    w_ref = refs.pop(0)
    r_ref = take(has_res)
    cw_ref = take(conv is not None)
    gain_ref = take(norm_out)
    ssq_in_ref = take(norm_in is not None)
    side_in_ref = take(has_side)
    o_ref = refs.pop(0)
    hb_ref, ssq_out_ref = take(norm_out), take(norm_out)
    side_out_ref = take(has_side)
    wb_ref = refs.pop(0)
    xext_ref = take(conv is not None)
    wf_ref, sem = take(prefetch is not None), take(prefetch is not None)
    j = pl.program_id(0)
    i = pl.program_id(1)
    nj = pl.num_programs(0)

    if has_side:
        side_out_ref[...] = side_in_ref[...].astype(BF16)

    if prefetch is None:
        @pl.when(i == 0)
        def _():
            wb_ref[...] = w_ref[...].astype(BF16)
    else:
        layer, w_tile_of = prefetch
        blk = wf_ref.shape[0] if w_t else wf_ref.shape[1]

        def fetch(jj):
            start = pl.multiple_of(w_tile_of(jj) * blk, blk)
            src = w_ref.at[layer, pl.ds(start, blk), :] if w_t else w_ref.at[layer, :, pl.ds(start, blk)]
            return pltpu.make_async_copy(src, wf_ref, sem)

        @pl.when(i == 0)
        def _():
            @pl.when(j == 0)
            def _():
                fetch(j).start()

            fetch(j).wait()
            wb_ref[...] = wf_ref[...].astype(BF16)

            @pl.when(j + 1 < nj)
            def _():
                fetch(j + 1).start()

    def matmul(rows):
        k0 = 0
        acc = None
        for a_ref in a_refs:
            k1 = k0 + a_ref.shape[1]
            if w_t:
                part = _dot_nt(a_ref[rows, :], wb_ref[:, k0:k1])
            else:
                part = _dot(a_ref[rows, :], wb_ref[k0:k1, :])
            acc = part if acc is None else acc + part
            k0 = k1
        return acc

    tm, tn = o_ref.shape
    if conv is None:
        acc = matmul(slice(0, tm))
        if norm_in is not None:
            ssq = jnp.sum(ssq_in_ref[...], axis=0)
            acc = acc * lax.rsqrt(jnp.sum(ssq, axis=-1, keepdims=True) * (1.0 / norm_in) + NORM_EPS)
        if relu2:
            acc = jnp.square(jnp.maximum(acc, 0.0))
        if has_res:
            acc = r_ref[...] + acc
        o_ref[...] = acc.astype(o_ref.dtype)
        if norm_out:
            hb_ref[...] = (acc * gain_ref[...]).astype(BF16)
            sq = acc * acc
            part = sq[:, :LANES]
            for c0 in range(LANES, tn, LANES):
                part = part + sq[:, c0:c0 + LANES]
            ssq_out_ref[...] = part
        return

    seq_len, l2_head_dim, n_scaled_tiles, scale = conv
    pad = SUBLANES
    rc = min(CONV_ROW_CHUNK, tm)

    @pl.when((i * tm) % seq_len == 0)
    def _():
        xext_ref[:pad, :] = jnp.zeros((pad, tn), F32)

    cw = cw_ref[...]
    mult = jnp.where(pl.program_id(0) < n_scaled_tiles, scale, 1.0)
    for r0 in range(0, tm, rc):
        acc = matmul(slice(r0, r0 + rc))
        xext_ref[pad + r0:pad + r0 + rc, :] = acc
        y = acc * cw[CONV_WIDTH - 1:CONV_WIDTH, :]
        for s in range(1, CONV_WIDTH):
            y = y + xext_ref[pad + r0 - s:pad + r0 - s + rc, :] * cw[CONV_WIDTH - 1 - s:CONV_WIDTH - s, :]
        y = _silu(y)
        if l2_head_dim:
            parts = []
            for h0 in range(0, tn, l2_head_dim):
                yh = y[:, h0:h0 + l2_head_dim]
                parts.append(yh * (lax.rsqrt(jnp.sum(yh * yh, axis=-1, keepdims=True) + NORM_EPS) * mult))
            y = jnp.concatenate(parts, axis=-1)
        o_ref[r0:r0 + rc, :] = y.astype(o_ref.dtype)
    xext_ref[:pad, :] = xext_ref[tm:tm + pad, :]


def _matmul(a_list, w, layer, out_dtype, *, n_tiles=None, w_tile_of=lambda j: j, w_t=False, res=None,
            relu2=False, conv=None, conv_w=None, conv_tile0=0, prefetch=False, side=None, norm_gain=None,
            row_ssq=None, tm=MATMUL_TM, tn=MATMUL_TN, name="matmul"):
    m = a_list[0].shape[0]
    k = sum(a.shape[1] for a in a_list)
    n = w.shape[1] if w_t else w.shape[2]
    assert (w.shape[2] if w_t else w.shape[1]) == k
    tm, tn = min(tm, m), min(tn, n)
    if n_tiles is None:
        assert n % tn == 0
        n_tiles = n // tn
    assert m % tm == 0
    in_specs = [pl.BlockSpec((tm, a.shape[1]), lambda j, i: (i, 0)) for a in a_list]
    w_block = (tn, k) if w_t else (k, tn)
    if prefetch:
        in_specs.append(pl.BlockSpec(memory_space=pl.ANY))
    elif w_t:
        in_specs.append(pl.BlockSpec((None, tn, k), lambda j, i: (layer, w_tile_of(j), 0)))
    else:
        in_specs.append(pl.BlockSpec((None, k, tn), lambda j, i: (layer, 0, w_tile_of(j))))
    scratch = [pltpu.VMEM(w_block, BF16)]
    args = list(a_list) + [w]
    if res is not None:
        in_specs.append(pl.BlockSpec((tm, tn), lambda j, i: (i, j)))
        args.append(res)
    if conv is not None:
        assert res is None and conv[0] % tm == 0
        in_specs.append(pl.BlockSpec((CONV_WIDTH, tn), lambda j, i: (0, conv_tile0 + j)))
        args.append(conv_w)
        scratch.append(pltpu.VMEM((tm + SUBLANES, tn), F32))
    if prefetch:
        scratch += [pltpu.VMEM(w_block, F32), pltpu.SemaphoreType.DMA(())]
    nm = m // tm
    out_specs = [pl.BlockSpec((tm, tn), lambda j, i: (i, j))]
    out_shape = [jax.ShapeDtypeStruct((m, n_tiles * tn), out_dtype)]
    if norm_gain is not None:
        assert conv is None
        in_specs.append(pl.BlockSpec((1, tn), lambda j, i: (0, j)))
        args.append(norm_gain.reshape(1, n_tiles * tn))
        out_specs += [pl.BlockSpec((tm, tn), lambda j, i: (i, j)),
                      pl.BlockSpec((None, tm, LANES), lambda j, i: (j, i, 0))]
        out_shape += [jax.ShapeDtypeStruct((m, n_tiles * tn), BF16),
                      jax.ShapeDtypeStruct((n_tiles, m, LANES), F32)]
    if row_ssq is not None:
        assert conv is None and row_ssq.shape[1:] == (m, LANES)
        in_specs.append(pl.BlockSpec((row_ssq.shape[0], tm, LANES), lambda j, i: (0, i, 0)))
        args.append(row_ssq)
    if side is not None:
        rows, cols = side.shape
        steps = n_tiles * nm
        assert rows % steps == 0 and (rows // steps) % 16 == 0
        side_spec = pl.BlockSpec((rows // steps, cols), lambda j, i: (j * nm + i, 0))
        in_specs.append(side_spec)
        args.append(side)
        out_specs.append(side_spec)
        out_shape.append(jax.ShapeDtypeStruct((rows, cols), BF16))
    outs = pl.pallas_call(
        functools.partial(_mm_kernel, n_a=len(a_list), has_res=res is not None, relu2=relu2, conv=conv,
                          w_t=w_t, prefetch=(layer, w_tile_of) if prefetch else None,
                          has_side=side is not None, norm_out=norm_gain is not None,
                          norm_in=None if row_ssq is None else k),
        grid=(n_tiles, nm),
        in_specs=in_specs,
        out_specs=out_specs,
        out_shape=out_shape,
        scratch_shapes=scratch,
        compiler_params=_params("arbitrary" if prefetch else "parallel", "arbitrary"),
        name=name,
    )(*args)
    return outs[0] if len(outs) == 1 else tuple(outs)


def _mm_acc_res_kernel(a_ref, w_ref, r_ref, o_ref):
    k = pl.program_id(2)

    @pl.when(k == 0)
    def _():
        o_ref[...] = r_ref[...] + _dot(a_ref[...], w_ref[...])

    @pl.when(k > 0)
    def _():
        o_ref[...] += _dot(a_ref[...], w_ref[...])


def _matmul_acc_res(a, w, res, *, tm=1024, tn=1024, tk=4096, name="matmul_acc_res"):
    m, k = a.shape
    n = w.shape[1]
    tm, tn, tk = min(tm, m), min(tn, n), min(tk, k)
    return pl.pallas_call(
        _mm_acc_res_kernel,
        grid=(m // tm, n // tn, k // tk),
        in_specs=[pl.BlockSpec((tm, tk), lambda i, j, l: (i, l)),
                  pl.BlockSpec((tk, tn), lambda i, j, l: (l, j)),
                  pl.BlockSpec((tm, tn), lambda i, j, l: (i, j))],
        out_specs=pl.BlockSpec((tm, tn), lambda i, j, l: (i, j)),
        out_shape=jax.ShapeDtypeStruct((m, n), F32),
        compiler_params=_params("parallel", "parallel", "arbitrary"),
        name=name,
    )(a, w, res)


def _retention_kernel(lg_ref, pos_ref, freq_ref, q_ref, k_ref, v_ref, g_ref, rn_ref,
                      o_ref, state_ref, dec_ref):
    first = (pl.program_id(0) == 0) & (pl.program_id(1) == 0)
    c = q_ref.shape[1]
    heads, dh = state_ref.shape[0], state_ref.shape[1]
    half = dh // 2
    lg = lg_ref[...]

    @pl.when(first)
    def _():
        row = lax.broadcasted_iota(jnp.int32, (c, c), 0)
        col = lax.broadcasted_iota(jnp.int32, (c, c), 1)
        rel = (row - col).astype(F32)
        for h in range(heads):
            dec_ref[h] = jnp.where(rel >= 0, jnp.exp(jnp.maximum(rel, 0.0) * lg[:, h:h + 1]), 0.0)

    @pl.when(pl.program_id(1) == 0)
    def _():
        state_ref[...] = jnp.zeros(state_ref.shape, F32)

    ang = pos_ref[0].astype(F32) * freq_ref[...]
    cos = jnp.cos(ang)
    sin = jnp.sin(ang)

    def rot(x):
        x1, x2 = x[:, :half], x[:, half:]
        return jnp.concatenate([x1 * cos - x2 * sin, x2 * cos + x1 * sin], axis=-1)

    idx = lax.broadcasted_iota(jnp.int32, (c, 1), 0).astype(F32)
    decay_q = jnp.exp((idx + 1.0) * lg)
    decay_k = jnp.exp((c - 1.0 - idx) * lg)
    decay_chunk = jnp.exp(c * lg)

    hs = range(heads)
    sl = [slice(h * dh, (h + 1) * dh) for h in hs]
    state = [state_ref[h] for h in hs]
    k = [rot(k_ref[0, :, sl[h]]) * (dh ** -0.5) for h in hs]
    qb = [rot(q_ref[0, :, sl[h]]).astype(BF16) for h in hs]
    vb = [v_ref[0, :, sl[h]].astype(BF16) for h in hs]
    scores = [_dot_nt(qb[h], k[h].astype(BF16)) for h in hs]
    inter = [_dot(qb[h], state[h].astype(BF16)) for h in hs]
    kv = [_dot_tn((k[h] * decay_k[:, h:h + 1]).astype(BF16), vb[h]) for h in hs]
    probs = [(scores[h] * dec_ref[h]).astype(BF16) for h in hs]
    intra = [_dot(probs[h], vb[h]) for h in hs]
    for h in hs:
        state_ref[h] = decay_chunk[:, h:h + 1] * state[h] + kv[h]
        out = intra[h] + inter[h] * decay_q[:, h:h + 1]
        y = out * lax.rsqrt(jnp.mean(out * out, axis=-1, keepdims=True) + NORM_EPS)
        y = y * rn_ref[:, sl[h]]
        o_ref[0, :, sl[h]] = (_silu(g_ref[0, :, sl[h]]) * y).astype(o_ref.dtype)


def _retention(proj, positions, ret_norm, *, heads, chunk=256):
    b, s, _ = proj.shape
    dh = RET_HEAD_DIM
    half = dh // 2
    wr = heads * dh
    chunk = min(chunk, s)
    log_gamma = jnp.log1p(-jnp.exp2(-5.0 - jnp.arange(heads, dtype=F32))).reshape(1, heads)
    inv_freq = (ROPE_THETA ** (-jnp.arange(half, dtype=F32) / half)).reshape(1, half)
    blk = lambda off: pl.BlockSpec((1, chunk, wr), lambda bi, ni: (bi, ni, off))
    const = lambda shape: pl.BlockSpec(shape, lambda bi, ni: (0, 0))
    return pl.pallas_call(
        _retention_kernel,
        grid=(b, s // chunk),
        in_specs=[const((1, heads)),
                  pl.BlockSpec((1, chunk, 1), lambda bi, ni: (bi, ni, 0)),
                  const((1, half)),
                  blk(0), blk(1), blk(2), blk(3),
                  const((1, wr))],
        out_specs=pl.BlockSpec((1, chunk, wr), lambda bi, ni: (bi, ni, 0)),
        out_shape=jax.ShapeDtypeStruct((b, s, wr), BF16),
        scratch_shapes=[pltpu.VMEM((heads, dh, dh), F32),
                        pltpu.VMEM((heads, chunk, chunk), F32)],
        compiler_params=_params("arbitrary", "arbitrary"),
        name="retention",
    )(log_gamma, positions.reshape(b, s, 1), inv_freq, proj, proj, proj, proj, ret_norm.reshape(1, wr))


def _block_tril(n, blk, dtype):
    row = lax.broadcasted_iota(jnp.int32, (n, n), 0)
    col = lax.broadcasted_iota(jnp.int32, (n, n), 1)
    return jnp.where((row >= col) & (row // blk == col // blk), 1.0, 0.0).astype(dtype)


def _bf16_pieces(x):
    hi = x.astype(BF16)
    r1 = x - hi.astype(F32)
    mid = r1.astype(BF16)
    lo = (r1 - mid.astype(F32)).astype(BF16)
    return hi, mid, lo


def _softplus(x):
    return jnp.maximum(x, 0.0) + jnp.log1p(jnp.exp(-jnp.abs(x)))


def _gdn_gates_kernel(hn_ref, wt_ref, a_ref, dt_ref, at_ref, dtt_ref, gcol_ref, grow_ref):
    nh = a_ref.shape[1]
    tm = hn_ref.shape[0]
    hn = hn_ref[...]
    ab = _dot_nt(hn, wt_ref[...])
    abt = _dot_nt(wt_ref[...], hn)
    ltri = _block_tril(tm, GDN_CHUNK, BF16)
    ld = -jnp.exp(a_ref[...]) * _softplus(ab[:, :nh] + dt_ref[...])
    ldt = -jnp.exp(at_ref[...]) * _softplus(abt[:nh, :] + dtt_ref[...])
    g = sum(_dot(ltri, piece) for piece in _bf16_pieces(ld))
    gt = sum(_dot_nt(piece, ltri) for piece in _bf16_pieces(ldt))
    gcol_ref[...] = jnp.concatenate([g, jax.nn.sigmoid(ab[:, nh:])], axis=-1)
    grow_ref[...] = gt


def _gdn_gates(hn, w_ab_t, a_log, dt_bias, *, tm=256):
    t, d = hn.shape
    nh = a_log.shape[0]
    tm = min(tm, t)
    const = lambda shape: pl.BlockSpec(shape, lambda i: (0, 0))
    return pl.pallas_call(
        _gdn_gates_kernel,
        grid=(t // tm,),
        in_specs=[pl.BlockSpec((tm, d), lambda i: (i, 0)),
                  const((2 * nh, d)),
                  const((1, nh)), const((1, nh)), const((nh, 1)), const((nh, 1))],
        out_specs=[pl.BlockSpec((tm, 2 * nh), lambda i: (i, 0)),
                   pl.BlockSpec((nh, tm), lambda i: (0, i))],
        out_shape=[jax.ShapeDtypeStruct((t, 2 * nh), F32),
                   jax.ShapeDtypeStruct((nh, t), F32)],
        compiler_params=_params("parallel"),
        name="gdn_gates",
    )(hn, w_ab_t, a_log.reshape(1, nh), dt_bias.reshape(1, nh),
      a_log.reshape(nh, 1), dt_bias.reshape(nh, 1))


def _gdn_kernel(q_ref, k_ref, v_ref, z_ref, gcol_ref, grow_ref, gn_ref, o_ref, state_ref, *, sub):
    n = pl.program_id(1)
    hg = pl.program_id(2)
    tb, wg = q_ref.shape[1], q_ref.shape[2]
    d = GDN_HEAD_DIM
    group = wg // d
    nh = grow_ref.shape[0]
    c = GDN_CHUNK

    @pl.when(n == 0)
    def _():
        for j in range(group):
            state_ref[hg * group + j] = jnp.zeros((d, d), F32)

    q_all = q_ref[0]
    k_all = k_ref[0]
    v_all = v_ref[0]
    z_all = z_ref[0]
    gcol = gcol_ref[0]
    lane = lax.broadcasted_iota(jnp.int32, gcol.shape, 1)

    row = lax.broadcasted_iota(jnp.int32, (sub, sub), 0)
    col = lax.broadcasted_iota(jnp.int32, (sub, sub), 1)
    incl = ((row // c) == (col // c)) & (row >= col)
    diag = row == col
    levels = int(math.log2(c))

    heads_idx = [hg * group + j for j in range(group)]
    states = [state_ref[h] for h in heads_idx]
    nsub = tb // sub
    cps = sub // c

    probs = []
    for j, h in enumerate(heads_idx):
        sl = slice(j * d, (j + 1) * d)
        q, k, v = q_all[:, sl], k_all[:, sl], v_all[:, sl]
        g_col = jnp.sum(jnp.where(lane == h, gcol, 0.0), axis=-1, keepdims=True)
        beta = jnp.sum(jnp.where(lane == nh + h, gcol, 0.0), axis=-1, keepdims=True)
        g_row = grow_ref[pl.ds(h, 1), :]
        for bi in range(nsub):
            rs = slice(bi * sub, (bi + 1) * sub)
            probs.append(dict(q=q[rs], k=k[rs], v=v[rs], beta=beta[rs], g_col=g_col[rs],
                              g_row=g_row[:, rs]))

    for pr in probs:
        pr["decay"] = jnp.where(incl, jnp.exp(jnp.where(incl, pr["g_col"] - pr["g_row"], 0.0)), 0.0)
        pr["eg"] = jnp.exp(pr["g_col"])
        pr["k_beta"] = pr["k"] * pr["beta"]
    for pr in probs:
        pr["kq"] = _dot_nt(jnp.concatenate([pr["k_beta"], pr["q"]], axis=0).astype(BF16),
                           pr["k"].astype(BF16))
    for pr in probs:
        pr["qkb"] = (pr["kq"][sub:] * pr["decay"]).astype(BF16)
        pr["p"] = jnp.where(diag, 0.0, -(pr["kq"][:sub] * pr["decay"]))
        pr["rhs"] = jnp.concatenate([pr["v"] * pr["beta"], pr["k_beta"] * pr["eg"]], axis=-1).astype(BF16)
    for pr in probs:
        pr["y"] = jnp.where(diag, 1.0, pr["p"])
    for lev in range(1, levels):
        for pr in probs:
            pb = pr["p"].astype(BF16)
            if lev == 1:
                pr["p"] = _dot(pb, pb)
                continue
            r = _dot(pb, jnp.concatenate([pb, pr["y"].astype(BF16)], axis=-1))
            pr["p"] = r[:, :sub]
            pr["y"] = pr["y"] + r[:, sub:]
    for pr in probs:
        pr["y"] = pr["y"] + _dot(pr["p"].astype(BF16), pr["y"].astype(BF16))
    for pr in probs:
        pr["uw"] = _dot(pr["y"].astype(BF16), pr["rhs"]).astype(BF16)
        pr["qk_uw"] = _dot(pr["qkb"], pr["uw"])
    for pr in probs:
        g_col = pr["g_col"]
        g_last = [g_col[(ci + 1) * c - 1:(ci + 1) * c] for ci in range(cps)]
        g_last_col = jnp.concatenate([jnp.broadcast_to(gl, (c, 1)) for gl in g_last], axis=0)
        pr["k_dec"] = (pr["k"] * jnp.exp(g_last_col - g_col)).astype(BF16)
        pr["e_last"] = [jnp.exp(gl) for gl in g_last]
        pr["q_eff"] = (pr["q"] * pr["eg"] - pr["qk_uw"][:, d:]).astype(BF16)
    chunks = [[] for _ in range(group)]
    for ci in range(cps):
        rows = slice(ci * c, (ci + 1) * c)
        for pi, pr in enumerate(probs):
            pr.setdefault("n_p", []).append(_dot_tn(pr["k_dec"][rows], pr["uw"][rows]))
    for pi, pr in enumerate(probs):
        for ci in range(cps):
            rows = slice(ci * c, (ci + 1) * c)
            n_p = pr["n_p"][ci]
            chunks[pi // nsub].append((n_p[:, :d], n_p[:, d:].astype(BF16), pr["q_eff"][rows],
                                       pr["qk_uw"][rows, :d], pr["e_last"][ci]))

    outs = [[] for _ in range(group)]
    for ci in range(tb // c):
        for j in range(group):
            n_c, p_c, q_c, o_c, e_c = chunks[j][ci]
            r = _dot(jnp.concatenate([p_c, q_c], axis=0), states[j].astype(BF16))
            outs[j].append(r[d:] + o_c)
            states[j] = e_c * states[j] - r[:d] + n_c

    ys = []
    for j, h in enumerate(heads_idx):
        state_ref[h] = states[j]
        out = jnp.concatenate(outs[j], axis=0)
        y = out * lax.rsqrt(jnp.mean(out * out, axis=-1, keepdims=True) + NORM_EPS)
        ys.append(y * gn_ref[...] * _silu(z_all[:, j * d:(j + 1) * d]))
    o_ref[0] = jnp.concatenate(ys, axis=-1).astype(o_ref.dtype)


def _gdn(qk, v, zproj, gcol, grow, gdn_norm, *, heads, z_col0, tb=256, group=8, sub=128):
    b, s, _ = qk.shape
    d = GDN_HEAD_DIM
    tb = min(tb, s)
    group = min(group, heads)
    wg = group * d
    assert heads % group == 0 and z_col0 % wg == 0
    ng = heads // group
    blk = lambda off: pl.BlockSpec((1, tb, wg), lambda bi, ni, hi: (bi, ni, off + hi))
    sub = min(sub, tb)
    assert tb % sub == 0 and sub % GDN_CHUNK == 0
    return pl.pallas_call(
        functools.partial(_gdn_kernel, sub=sub),
        grid=(b, s // tb, ng),
        in_specs=[blk(0), blk(ng), blk(0), blk(z_col0 // wg),
                  pl.BlockSpec((1, tb, 2 * heads), lambda bi, ni, hi: (bi, ni, 0)),
                  pl.BlockSpec((heads, tb), lambda bi, ni, hi: (0, bi * (s // tb) + ni)),
                  pl.BlockSpec((1, d), lambda bi, ni, hi: (0, 0))],
        out_specs=pl.BlockSpec((1, tb, wg), lambda bi, ni, hi: (bi, ni, hi)),
        out_shape=jax.ShapeDtypeStruct((b, s, heads * d), BF16),
        scratch_shapes=[pltpu.VMEM((heads, d, d), F32)],
        compiler_params=_params("arbitrary", "arbitrary", "arbitrary"),
        name="gdn",
    )(qk, qk, v, zproj, gcol, grow, gdn_norm.reshape(1, d))


def _xattn_kernel(q_ref, k_ref, v_ref, o_ref, *, heads):
    d = q_ref.shape[2]
    dh = d // heads
    sl = [slice(h * dh, (h + 1) * dh) for h in range(heads)]
    scores = [_dot_nt(q_ref[0, :, s], k_ref[0, :, s]) * (dh ** -0.5) for s in sl]
    probs = []
    for sc in scores:
        e = jnp.exp(sc - jnp.max(sc, axis=-1, keepdims=True))
        probs.append((e / jnp.sum(e, axis=-1, keepdims=True)).astype(BF16))
    for s, p in zip(sl, probs):
        o_ref[0, :, s] = _dot(p, v_ref[0, :, s]).astype(o_ref.dtype)


def _xattn(q, k, v, *, heads, tm=512):
    b, s, d = q.shape
    m = k.shape[1]
    tm = min(tm, s)
    return pl.pallas_call(
        functools.partial(_xattn_kernel, heads=heads),
        grid=(b, s // tm),
        in_specs=[pl.BlockSpec((1, tm, d), lambda bi, i: (bi, i, 0)),
                  pl.BlockSpec((1, m, d), lambda bi, i: (bi, 0, 0)),
                  pl.BlockSpec((1, m, d), lambda bi, i: (bi, 0, 0))],
        out_specs=pl.BlockSpec((1, tm, d), lambda bi, i: (bi, i, 0)),
        out_shape=jax.ShapeDtypeStruct((b, s, d), BF16),
        compiler_params=_params("parallel", "parallel"),
        name="xattn",
    )(q, k, v)


def _token_mixer(hn, positions, w_in, l, ret_norm, gdn_conv, gdn_a_log, gdn_dt_bias, gdn_norm,
                 *, tm=MATMUL_TM, tn=MATMUL_TN_WIDE):
    b, s = positions.shape
    ret_heads = ret_norm.shape[0]
    gdn_heads = gdn_a_log.shape[0]
    ret_width = ret_heads * RET_HEAD_DIM
    gdn_width = gdn_heads * GDN_HEAD_DIM
    main_width = 4 * ret_width + 4 * gdn_width
    assert (4 * ret_width) % tn == 0 and gdn_width % tn == 0
    t_ret, t_gdn = 4 * ret_width // tn, gdn_width // tn
    w_in_t = jnp.swapaxes(w_in, 1, 2)
    proj = _matmul([hn], w_in_t, l, F32, n_tiles=t_ret + t_gdn, tm=tm, tn=tn, w_t=True, prefetch=True,
                   w_tile_of=lambda j: jnp.where(j < t_ret, j, j + 3 * t_gdn), name="in_proj")
    qk = _matmul([hn], w_in_t, l, F32, n_tiles=2 * t_gdn, w_tile_of=lambda j: t_ret + j, tm=tm, tn=tn,
                 w_t=True, prefetch=True, conv=(s, GDN_HEAD_DIM, t_gdn, GDN_HEAD_DIM ** -0.5),
                 conv_w=gdn_conv, name="in_proj_gdn_qk")
    gv = _matmul([hn], w_in_t, l, F32, n_tiles=t_gdn, w_tile_of=lambda j: t_ret + 2 * t_gdn + j,
                 tm=tm, tn=tn, w_t=True, prefetch=True, conv=(s, 0, 0, 1.0), conv_w=gdn_conv,
                 conv_tile0=2 * t_gdn, name="in_proj_gdn_v")
    w_ab_t = w_in_t[l, main_width:, :].astype(BF16)
    gcol, grow = _gdn_gates(hn, w_ab_t, gdn_a_log, gdn_dt_bias)
    proj3 = proj.reshape(b, s, 4 * ret_width + gdn_width)
    ret = _retention(proj3, positions, ret_norm, heads=ret_heads)
    gdn = _gdn(qk.reshape(b, s, 2 * gdn_width), gv.reshape(b, s, gdn_width), proj3,
               gcol.reshape(b, s, 2 * gdn_heads), grow, gdn_norm,
               heads=gdn_heads, z_col0=4 * ret_width)
    return ret.reshape(b * s, ret_width), gdn.reshape(b * s, gdn_width)


def kernel(x, mem, positions, mix_norm, w_in, ret_norm, gdn_conv, gdn_a_log, gdn_dt_bias, gdn_norm,
           w_mix_out, xattn_norm, mem_norm, w_xq, w_xk, w_xv, w_xo, mlp_norm, w_up, w_down, final_norm):
    b, s, d = x.shape
    m = mem.shape[1]
    t = b * s
    depth = w_in.shape[0]

    h = x.reshape(t, d)
    for l in range(depth):
        hn = _rmsnorm(h, mix_norm[l], BF16, name="mix_rmsnorm")
        ret, gdn = _token_mixer(hn, positions, w_in, l, ret_norm[l], gdn_conv[l], gdn_a_log[l],
                                gdn_dt_bias[l], gdn_norm[l])
        h, hg, ssq, w_down_bf16 = _matmul([ret, gdn], w_mix_out, l, F32, res=h, norm_gain=xattn_norm[l],
                                          side=w_down[l], tm=MATMUL_TM_RES, tn=MATMUL_TN_WIDE,
                                          prefetch=True, name="mix_out")

        mn = _rmsnorm(mem.reshape(b * m, d), mem_norm[l], BF16, name="mem_rmsnorm")
        q = _matmul([hg], w_xq, l, BF16, row_ssq=ssq, tn=MATMUL_TN_WIDE, prefetch=True, name="xattn_q")
        k = _matmul([mn], w_xk, l, BF16, name="xattn_k")
        v = _matmul([mn], w_xv, l, BF16, name="xattn_v")
        att = _xattn(q.reshape(b, s, d), k.reshape(b, m, d), v.reshape(b, m, d), heads=XATTN_HEADS)
        h, hg, ssq = _matmul([att.reshape(t, d)], w_xo, l, F32, res=h, norm_gain=mlp_norm[l],
                             tm=MATMUL_TM_RES, tn=MATMUL_TN_WIDE, prefetch=True, name="xattn_out")

        up = _matmul([hg], w_up, l, BF16, relu2=True, row_ssq=ssq, tn=MATMUL_TN_WIDE, prefetch=True,
                     name="mlp_up")
        h = _matmul_acc_res(up, w_down_bf16, h, name="mlp_down")
    out = _rmsnorm(h, final_norm, x.dtype, name="final_rmsnorm")
    return out.reshape(b, s, d)
```
